```python
import math
import jax, jax.numpy as jnp
from jax import lax
import numpy as np

D_MODEL = 2048
BATCH = 4
SEQ = 2048
DEPTH = 1

MIX_WIDTH = D_MODEL
SB_HEADS = 8
SB_HEAD_DIM = MIX_WIDTH // 2 // SB_HEADS
SB_WIDTH = SB_HEADS * SB_HEAD_DIM
SB_BLOCK = 128
GLA_HEADS = 4
GLA_V_DIM = (MIX_WIDTH - SB_WIDTH) // GLA_HEADS
GLA_K_DIM = GLA_V_DIM // 2
GLA_V_WIDTH = GLA_HEADS * GLA_V_DIM
GLA_K_WIDTH = GLA_HEADS * GLA_K_DIM
GLA_GATE_RANK = 16
GLA_GATE_TAU = 16.0
GLA_CHUNK = 64
IN_COLS = 3 * SB_WIDTH + 2 * GLA_K_WIDTH + 2 * GLA_V_WIDTH + GLA_GATE_RANK
MEM_LEN = 256
CROSS_HEADS = 4
CROSS_HEAD_DIM = D_MODEL // CROSS_HEADS
N_EXPERTS = 32
TOP_K = 4
D_FF = D_MODEL
SWIGLU_LIMIT = 7.0
SWIGLU_ALPHA = 1.702
MOE_BLOCK = 128
EPS = 1e-6

kernel_name = 'hybrid_stickbreak_gla_moe_block'


def rms_norm(x, g):
    xf = x.astype(jnp.float32)
    y = xf * lax.rsqrt(jnp.mean(xf * xf, axis=-1, keepdims=True) + EPS)
    return (y * g.astype(jnp.float32)).astype(x.dtype)


def split_heads(t, n_heads):
    b, s, _ = t.shape
    return t.reshape(b, s, n_heads, -1).transpose(0, 2, 1, 3)


def head_rms_norm(o, g):
    b, h, s, d = o.shape
    of = o.astype(jnp.float32)
    y = of * lax.rsqrt(jnp.mean(of * of, axis=-1, keepdims=True) + EPS)
    y = y * g.astype(jnp.float32).reshape(1, h, 1, d)
    return y.transpose(0, 2, 1, 3).reshape(b, s, h * d).astype(o.dtype)


def stick_breaking_attention(q, k, v):
    s_len, dh = q.shape[2], q.shape[3]
    scale = dh ** -0.5
    outs = []
    for start in range(0, s_len, SB_BLOCK):
        end = start + SB_BLOCK
        z = jnp.einsum('bhqd,bhkd->bhqk', q[:, :, start:end], k[:, :, :end]).astype(jnp.float32) * scale
        t_pos = start + jnp.arange(SB_BLOCK)[:, None]
        s_pos = jnp.arange(end)[None, :]
        causal = s_pos < t_pos
        log_beta = jax.nn.log_sigmoid(z)
        log_1m = jnp.where(causal, log_beta - z, 0.0)
        between = lax.cumsum(log_1m, axis=3, reverse=True) - log_1m
        a = jnp.where(causal, jnp.exp(log_beta + between), 0.0)
        outs.append(jnp.einsum('bhqk,bhkd->bhqd', a.astype(v.dtype), v[:, :, :end]))
    return jnp.concatenate(outs, axis=2)


def gla_chunked(q, k, v, log_a):
    b, h, s_len, dk = q.shape
    dv = v.shape[-1]
    c = GLA_CHUNK
    n = s_len // c
    f32 = jnp.float32
    qc = q.reshape(b, h, n, c, dk).astype(f32) * dk ** -0.5
    kc = k.reshape(b, h, n, c, dk).astype(f32)
    vc = v.reshape(b, h, n, c, dv).astype(f32)
    g = jnp.cumsum(log_a.reshape(b, h, n, c, dk).astype(f32), axis=3)
    q_in = qc * jnp.exp(g)
    k_in = kc * jnp.exp(-g)
    scores = jnp.einsum('bhncd,bhnsd->bhncs', q_in, k_in)
    mask = jnp.tril(jnp.ones((c, c), dtype=bool))
    scores = jnp.where(mask, scores, 0.0)
    o_intra = jnp.einsum('bhncs,bhnsv->bhncv', scores, vc)
    g_last = g[:, :, :, -1:, :]
    kv_chunk = jnp.einsum('bhncd,bhncv->bhndv', kc * jnp.exp(g_last - g), vc)
    decay = jnp.exp(g_last[:, :, :, 0, :])

    def step(state, inp):
        kv_n, decay_n = inp
        return decay_n[..., None] * state + kv_n, state

    init = jnp.zeros((b, h, dk, dv), f32)
    _, prev_states = lax.scan(step, init, (jnp.moveaxis(kv_chunk, 2, 0), jnp.moveaxis(decay, 2, 0)))
    prev_states = jnp.moveaxis(prev_states, 0, 2)
    o_inter = jnp.einsum('bhncd,bhndv->bhncv', q_in, prev_states)
    return (o_intra + o_inter).reshape(b, h, s_len, dv).astype(v.dtype)


def hybrid_mixer(h, w_in, w_gla_gate_up, b_gla_gate, sb_norm_g, gla_norm_g, w_out):
    proj = h @ w_in
    widths = [SB_WIDTH, SB_WIDTH, SB_WIDTH, GLA_K_WIDTH, GLA_K_WIDTH, GLA_V_WIDTH, GLA_V_WIDTH]
    cuts = list(np.cumsum(widths))
    sb_q, sb_k, sb_v, gla_q, gla_k, gla_v, gla_r, gla_low = jnp.split(proj, cuts, axis=-1)
    sb_o = stick_breaking_attention(split_heads(sb_q, SB_HEADS), split_heads(sb_k, SB_HEADS), split_heads(sb_v, SB_HEADS))
    sb_o = head_rms_norm(sb_o, sb_norm_g)
    log_a = jax.nn.log_sigmoid((gla_low @ w_gla_gate_up + b_gla_gate).astype(jnp.float32)) / GLA_GATE_TAU
    gla_o = gla_chunked(split_heads(gla_q, GLA_HEADS), split_heads(gla_k, GLA_HEADS), split_heads(gla_v, GLA_HEADS), split_heads(log_a, GLA_HEADS))
    gla_o = head_rms_norm(gla_o, gla_norm_g) * jax.nn.silu(gla_r)
    return jnp.concatenate([sb_o, gla_o], axis=-1) @ w_out


def memory_cross_attention(h, mem_n, w_cq, w_ckv, w_co):
    b, s_len, _ = h.shape
    q = (h @ w_cq).reshape(b, s_len, CROSS_HEADS, CROSS_HEAD_DIM)
    k, v = jnp.split(mem_n @ w_ckv, 2, axis=-1)
    k = k.reshape(b, -1, CROSS_HEADS, CROSS_HEAD_DIM)
    v = v.reshape(b, -1, CROSS_HEADS, CROSS_HEAD_DIM)
    scores = jnp.einsum('bqhd,bkhd->bhqk', q, k).astype(jnp.float32) * CROSS_HEAD_DIM ** -0.5
    p = jax.nn.softmax(scores, axis=-1).astype(v.dtype)
    o = jnp.einsum('bhqk,bkhd->bqhd', p, v).reshape(b, s_len, D_MODEL)
    return o @ w_co


def moe_ffn(h, w_router, b_router, w_gate_up, b_gate_up, w_down, b_down):
    b, s_len, d = h.shape
    n_tok = b * s_len
    hf = h.reshape(n_tok, d)
    logits = (hf @ w_router + b_router).astype(jnp.float32)
    top_logits, top_idx = lax.top_k(logits, TOP_K)
    top_w = jax.nn.softmax(top_logits, axis=-1)
    n_assign = n_tok * TOP_K
    expert_flat = top_idx.reshape(n_assign)
    token_flat = jnp.arange(n_assign, dtype=jnp.int32) // TOP_K
    weight_flat = top_w.reshape(n_assign)
    order = jnp.argsort(expert_flat)
    sorted_expert = expert_flat[order]
    counts = jnp.bincount(expert_flat, length=N_EXPERTS)
    padded = ((counts + MOE_BLOCK - 1) // MOE_BLOCK) * MOE_BLOCK
    start = jnp.cumsum(counts) - counts
    pend = jnp.cumsum(padded)
    pstart = pend - padded
    dest = pstart[sorted_expert] + (jnp.arange(n_assign) - start[sorted_expert])
    n_blocks = -(-(n_assign + N_EXPERTS * (MOE_BLOCK - 1)) // MOE_BLOCK)
    n_slots = n_blocks * MOE_BLOCK
    slot_token = jnp.full((n_slots,), n_tok, jnp.int32).at[dest].set(token_flat[order])
    slot_weight = jnp.zeros((n_slots,), jnp.float32).at[dest].set(weight_flat[order])
    block_expert = jnp.minimum(jnp.searchsorted(pend, jnp.arange(n_blocks) * MOE_BLOCK, side='right'), N_EXPERTS - 1)
    h_pad = jnp.concatenate([hf, jnp.zeros((1, d), hf.dtype)], axis=0)
    xs = h_pad[slot_token].reshape(n_blocks, MOE_BLOCK, d)

    def expert_block(args):
        xb, e = args
        gu = xb @ w_gate_up[e] + b_gate_up[e]
        gate, lin = jnp.split(gu, 2, axis=-1)
        gate = jnp.minimum(gate, SWIGLU_LIMIT)
        lin = jnp.clip(lin, -SWIGLU_LIMIT, SWIGLU_LIMIT)
        act = gate * jax.nn.sigmoid(SWIGLU_ALPHA * gate) * (lin + 1.0)
        return act @ w_down[e] + b_down[e]

    ys = lax.map(expert_block, (xs, block_expert)).reshape(n_slots, d)
    ys = ys * slot_weight[:, None].astype(ys.dtype)
    out = jax.ops.segment_sum(ys, slot_token, num_segments=n_tok + 1)[:n_tok]
    return out.reshape(b, s_len, d)


def setup_inputs(seed: int = 0) -> dict:
    key = jax.random.key(seed)
    ks = jax.random.split(key, 24)
    f32 = jnp.float32
    L = DEPTH

    def nrm(k, shape, scale):
        return jax.random.normal(k, shape, f32) * scale

    def gain(k, shape):
        return 1.0 + 0.02 * jax.random.normal(k, shape, f32)

    return {
        'x': nrm(ks[0], (BATCH, SEQ, D_MODEL), 1.0),
        'mem': nrm(ks[1], (BATCH, MEM_LEN, D_MODEL), 1.0),
        'norm_mix_g': gain(ks[2], (L, D_MODEL)),
        'w_in': nrm(ks[3], (L, D_MODEL, IN_COLS), D_MODEL ** -0.5),
        'w_gla_gate_up': nrm(ks[4], (L, GLA_GATE_RANK, GLA_K_WIDTH), GLA_GATE_RANK ** -0.5),
        'b_gla_gate': nrm(ks[5], (L, GLA_K_WIDTH), 0.1),
        'sb_norm_g': gain(ks[6], (L, SB_WIDTH)),
        'gla_norm_g': gain(ks[7], (L, GLA_V_WIDTH)),
        'w_out': nrm(ks[8], (L, MIX_WIDTH, D_MODEL), MIX_WIDTH ** -0.5),
        'norm_cross_g': gain(ks[9], (L, D_MODEL)),
        'norm_mem_g': gain(ks[10], (L, D_MODEL)),
        'w_cq': nrm(ks[11], (L, D_MODEL, D_MODEL), D_MODEL ** -0.5),
        'w_ckv': nrm(ks[12], (L, D_MODEL, 2 * D_MODEL), D_MODEL ** -0.5),
        'w_co': nrm(ks[13], (L, D_MODEL, D_MODEL), D_MODEL ** -0.5),
        'norm_moe_g': gain(ks[14], (L, D_MODEL)),
        'w_router': nrm(ks[15], (L, D_MODEL, N_EXPERTS), D_MODEL ** -0.5),
        'b_router': nrm(ks[16], (L, N_EXPERTS), 0.01),
        'w_gate_up': nrm(ks[17], (L, N_EXPERTS, D_MODEL, 2 * D_FF), D_MODEL ** -0.5),
        'b_gate_up': nrm(ks[18], (L, N_EXPERTS, 2 * D_FF), 0.02),
        'w_down': nrm(ks[19], (L, N_EXPERTS, D_FF, D_MODEL), D_FF ** -0.5),
        'b_down': nrm(ks[20], (L, N_EXPERTS, D_MODEL), 0.02),
        'norm_final_g': gain(ks[21], (D_MODEL,)),
    }


def reference(x, mem, norm_mix_g, w_in, w_gla_gate_up, b_gla_gate, sb_norm_g, gla_norm_g, w_out,
              norm_cross_g, norm_mem_g, w_cq, w_ckv, w_co,
              norm_moe_g, w_router, b_router, w_gate_up, b_gate_up, w_down, b_down,
              norm_final_g):
    for layer in range(DEPTH):
        h = rms_norm(x, norm_mix_g[layer])
        x = x + hybrid_mixer(h, w_in[layer], w_gla_gate_up[layer], b_gla_gate[layer],
                             sb_norm_g[layer], gla_norm_g[layer], w_out[layer])
        h = rms_norm(x, norm_cross_g[layer])
        mem_n = rms_norm(mem, norm_mem_g[layer])
        x = x + memory_cross_attention(h, mem_n, w_cq[layer], w_ckv[layer], w_co[layer])
        h = rms_norm(x, norm_moe_g[layer])
        x = x + moe_ffn(h, w_router[layer], b_router[layer], w_gate_up[layer], b_gate_up[layer],
                        w_down[layer], b_down[layer])
    return rms_norm(x, norm_final_g)
```

```python
import functools

import jax
import jax.numpy as jnp
from jax import lax
from jax.experimental import pallas as pl
from jax.experimental.pallas import tpu as pltpu

F32 = jnp.float32
BF16 = jnp.bfloat16
I32 = jnp.int32
U32 = jnp.uint32

D_MODEL = 2048
SB_HEADS = 8
SB_HEAD_DIM = 128
SB_WIDTH = SB_HEADS * SB_HEAD_DIM
GLA_HEADS = 4
GLA_K_DIM = 128
GLA_V_DIM = 256
GLA_K_WIDTH = GLA_HEADS * GLA_K_DIM
GLA_V_WIDTH = GLA_HEADS * GLA_V_DIM
GLA_GATE_RANK = 16
GLA_GATE_TAU = 16.0
GLA_CHUNK = 64
PROJ_MAIN = 3 * SB_WIDTH + 2 * GLA_K_WIDTH + 2 * GLA_V_WIDTH
CROSS_HEADS = 4
CROSS_HEAD_DIM = D_MODEL // CROSS_HEADS
N_EXPERTS = 32
TOP_K = 4
D_FF = D_MODEL
SWIGLU_LIMIT = 7.0
SWIGLU_ALPHA = 1.702
EPS = 1e-6

LANES = 128
VMEM_LIMIT_BYTES = 56 * 1024 * 1024

_SBQ_BLK = 0
_SBK_BLK = SB_WIDTH // LANES
_SBV_BLK = 2 * SB_WIDTH // LANES
_GQ_BLK = 3 * SB_WIDTH // LANES
_GK_BLK = _GQ_BLK + GLA_K_WIDTH // LANES
_GV_BLK256 = (3 * SB_WIDTH + 2 * GLA_K_WIDTH) // GLA_V_DIM
_GR_BLK256 = _GV_BLK256 + GLA_V_WIDTH // GLA_V_DIM

SB_ZERO_LOG = -104.0

MOE_CHUNK = 1280
MOE_SUB = 256
MOE_TN = 256
COPY_ROWS_PER_STEP = 2048

_NT = (((1,), (1,)), ((), ()))
_TN = (((0,), (0,)), ((), ()))


def _params(*sem):
    return pltpu.CompilerParams(dimension_semantics=sem, vmem_limit_bytes=VMEM_LIMIT_BYTES)


def _rms(x, g):
    return x * lax.rsqrt(jnp.mean(x * x, axis=-1, keepdims=True) + EPS) * g


def _log_sigmoid(x):
    return -(jnp.maximum(-x, 0.0) + jnp.log1p(jnp.exp(-jnp.abs(x))))


def _split_bf16(x):
    hi = x.astype(BF16)
    lo = (x - hi.astype(F32)).astype(BF16)
    return hi, lo


def _dot(a, b):
    return jnp.dot(a, b, preferred_element_type=F32)


def _pack_halves(x):
    n = x.shape[1] // 2
    lo = lax.bitcast_convert_type(x[:, :n].astype(F32), U32)
    hi = lax.bitcast_convert_type(x[:, n:].astype(F32), U32)
    return jnp.bitwise_or(jnp.right_shift(lo, jnp.uint32(16)), jnp.bitwise_and(hi, jnp.uint32(0xFFFF0000)))


def _unpack_halves(w):
    lo = lax.bitcast_convert_type(jnp.left_shift(w, jnp.uint32(16)), F32)
    hi = lax.bitcast_convert_type(jnp.bitwise_and(w, jnp.uint32(0xFFFF0000)), F32)
    return lo.astype(BF16), hi.astype(BF16)


def _norm_matmul_body(x_ref, g_ref, w_ref, o_ref, h_scr):
    @pl.when(pl.program_id(1) == 0)
    def _():
        h_scr[...] = _rms(x_ref[...], g_ref[...]).astype(BF16)

    o_ref[...] = _dot(h_scr[...], w_ref[...]).astype(o_ref.dtype)


def _norm_matmul(x, g, w, *, tm, tn, out_dtype):
    m, k = x.shape
    n = w.shape[1]
    return pl.pallas_call(
        _norm_matmul_body,
        grid=(m // tm, n // tn),
        in_specs=[
            pl.BlockSpec((tm, k), lambda i, j: (i, 0)),
            pl.BlockSpec((1, k), lambda i, j: (0, 0)),
            pl.BlockSpec((k, tn), lambda i, j: (0, j)),
        ],
        out_specs=pl.BlockSpec((tm, tn), lambda i, j: (i, j)),
        out_shape=jax.ShapeDtypeStruct((m, n), out_dtype),
        scratch_shapes=[pltpu.VMEM((tm, k), BF16)],
        compiler_params=_params("parallel", "arbitrary"),
        name="norm_matmul",
    )(x, g.reshape(1, k), w)


def _inproj_body(x_ref, g_ref, w_ref, wlow_ref, wup_ref, bup_ref, o_ref, la_ref, h_scr):
    @pl.when(pl.program_id(1) == 0)
    def _():
        h = _rms(x_ref[...], g_ref[...]).astype(BF16)
        h_scr[...] = h
        low = _dot(h, wlow_ref[...])
        pre = _dot(low.astype(BF16), wup_ref[...]) + bup_ref[...]
        la_ref[...] = _log_sigmoid(pre) / GLA_GATE_TAU

    o_ref[...] = _dot(h_scr[...], w_ref[...]).astype(o_ref.dtype)


def _inproj(x, g, w_main, w_low, w_up, b_up, *, tm, tn):
    m, k = x.shape
    n = w_main.shape[1]
    return pl.pallas_call(
        _inproj_body,
        grid=(m // tm, n // tn),
        in_specs=[
            pl.BlockSpec((tm, k), lambda i, j: (i, 0)),
            pl.BlockSpec((1, k), lambda i, j: (0, 0)),
            pl.BlockSpec((k, tn), lambda i, j: (0, j)),
            pl.BlockSpec((k, LANES), lambda i, j: (0, 0)),
            pl.BlockSpec((LANES, GLA_K_WIDTH), lambda i, j: (0, 0)),
            pl.BlockSpec((1, GLA_K_WIDTH), lambda i, j: (0, 0)),
        ],
        out_specs=[
            pl.BlockSpec((tm, tn), lambda i, j: (i, j)),
            pl.BlockSpec((tm, GLA_K_WIDTH), lambda i, j: (i, 0)),
        ],
        out_shape=[
            jax.ShapeDtypeStruct((m, n), BF16),
            jax.ShapeDtypeStruct((m, GLA_K_WIDTH), F32),
        ],
        scratch_shapes=[pltpu.VMEM((tm, k), BF16)],
        compiler_params=_params("parallel", "arbitrary"),
        name="inproj",
    )(x, g.reshape(1, k), w_main, w_low, w_up, b_up.reshape(1, -1))


def _sb_body(q_ref, k_ref, v_ref, g_ref, o_ref, acc_ref, carry_ref, *, tq):
    qi = pl.program_id(2)
    q = q_ref[0]
    scale = SB_HEAD_DIM ** -0.5
    row = lax.broadcasted_iota(I32, (tq, tq), 0)
    col = lax.broadcasted_iota(I32, (tq, tq), 1)
    causal = col < row
    later = (row > col).astype(BF16)

    def tile(kj, diag):
        off = pl.multiple_of(kj * tq, tq)
        kblk = k_ref[0, pl.ds(off, tq), :]
        vblk = v_ref[0, pl.ds(off, tq), :]
        z = lax.dot_general(q, kblk, _NT, preferred_element_type=F32) * scale
        lp = jnp.log1p(jnp.exp(-jnp.abs(z)))
        log_beta = jnp.minimum(z, 0.0) - lp
        log_1m = log_beta - z
        if diag:
            log_1m = jnp.where(causal, log_1m, 0.0)
        hi, lo = _split_bf16(log_1m)
        between = _dot(hi, later) + _dot(lo, later)
        carry = carry_ref[...]
        a = jnp.exp(log_beta + between + carry)
        if diag:
            a = jnp.where(causal, a, 0.0)
        acc_ref[...] += _dot(a.astype(BF16), vblk)
        carry = carry + jnp.sum(log_1m, axis=-1, keepdims=True)
        carry_ref[...] = carry
        return jnp.max(carry)

    acc_ref[...] = jnp.zeros_like(acc_ref)
    carry_ref[...] = jnp.zeros_like(carry_ref)
    top = tile(qi, True)

    def cond(state):
        kj, top = state
        return jnp.logical_and(kj >= 0, top > SB_ZERO_LOG)

    def body(state):
        kj, _ = state
        return kj - 1, tile(kj, False)

    lax.while_loop(cond, body, (qi - 1, top))
    o_ref[0] = _rms(acc_ref[...], g_ref[...]).astype(o_ref.dtype)


def _sb_attention(proj, g, *, tq):
    b, s, _ = proj.shape
    return pl.pallas_call(
        functools.partial(_sb_body, tq=tq),
        grid=(b, SB_HEADS, s // tq),
        in_specs=[
            pl.BlockSpec((1, tq, SB_HEAD_DIM), lambda bi, h, qi: (bi, qi, _SBQ_BLK + h)),
            pl.BlockSpec((1, s, SB_HEAD_DIM), lambda bi, h, qi: (bi, 0, _SBK_BLK + h)),
            pl.BlockSpec((1, s, SB_HEAD_DIM), lambda bi, h, qi: (bi, 0, _SBV_BLK + h)),
            pl.BlockSpec((1, SB_HEAD_DIM), lambda bi, h, qi: (0, h)),
        ],
        out_specs=pl.BlockSpec((1, tq, SB_HEAD_DIM), lambda bi, h, qi: (bi, qi, h)),
        out_shape=jax.ShapeDtypeStruct((b, s, SB_WIDTH), BF16),
        scratch_shapes=[pltpu.VMEM((tq, SB_HEAD_DIM), F32), pltpu.VMEM((tq, 1), F32)],
        compiler_params=_params("parallel", "parallel", "arbitrary"),
        name="sb_attention",
    )(proj, proj, proj, g.reshape(1, SB_WIDTH))


def _gla_body(q_ref, k_ref, v_ref, r_ref, la_ref, g_ref, o_ref, st_ref, *, rows):
    @pl.when(pl.program_id(2) == 0)
    def _():
        st_ref[...] = jnp.zeros_like(st_ref)

    shift = GLA_CHUNK.bit_length() - 1
    ri = lax.broadcasted_iota(I32, (rows, rows), 0)
    ci = lax.broadcasted_iota(I32, (rows, rows), 1)
    same = jnp.right_shift(ri, shift) == jnp.right_shift(ci, shift)
    tril = jnp.logical_and(same, ci <= ri)
    hi, lo = _split_bf16(la_ref[0])
    tril_b = tril.astype(BF16)
    same_b = same.astype(BF16)
    gcum = _dot(tril_b, hi) + _dot(tril_b, lo)
    glast = _dot(same_b, hi) + _dot(same_b, lo)

    q = q_ref[0].astype(F32) * (GLA_K_DIM ** -0.5)
    k = k_ref[0].astype(F32)
    v = v_ref[0]
    q_in = (q * jnp.exp(gcum)).astype(BF16)
    k_in = (k * jnp.exp(-gcum)).astype(BF16)
    k_out = (k * jnp.exp(glast - gcum)).astype(BF16)
    scores = lax.dot_general(q_in, k_in, _NT, preferred_element_type=F32)
    scores = jnp.where(tril, scores, 0.0)
    o_intra = _dot(scores.astype(BF16), v)

    state_t = st_ref[...]
    outs = []
    for n in range(rows // GLA_CHUNK):
        sl = slice(n * GLA_CHUNK, (n + 1) * GLA_CHUNK)
        o_inter = lax.dot_general(q_in[sl], state_t.astype(BF16), _NT, preferred_element_type=F32)
        outs.append(o_intra[sl] + o_inter)
        kv_t = lax.dot_general(v[sl], k_out[sl], _TN, preferred_element_type=F32)
        decay = jnp.exp(glast[n * GLA_CHUNK:n * GLA_CHUNK + 1, :])
        state_t = state_t * decay + kv_t
    st_ref[...] = state_t

    o = jnp.concatenate(outs, axis=0)
    gate = r_ref[0].astype(F32)
    y = _rms(o, g_ref[...]) * (gate * jax.nn.sigmoid(gate))
    o_ref[0] = y.astype(o_ref.dtype)


def _gla(proj, log_a, g, *, rows):
    b, s, _ = proj.shape
    return pl.pallas_call(
        functools.partial(_gla_body, rows=rows),
        grid=(b, GLA_HEADS, s // rows),
        in_specs=[
            pl.BlockSpec((1, rows, GLA_K_DIM), lambda bi, h, r: (bi, r, _GQ_BLK + h)),
            pl.BlockSpec((1, rows, GLA_K_DIM), lambda bi, h, r: (bi, r, _GK_BLK + h)),
            pl.BlockSpec((1, rows, GLA_V_DIM), lambda bi, h, r: (bi, r, _GV_BLK256 + h)),
            pl.BlockSpec((1, rows, GLA_V_DIM), lambda bi, h, r: (bi, r, _GR_BLK256 + h)),
            pl.BlockSpec((1, rows, GLA_K_DIM), lambda bi, h, r: (bi, r, h)),
            pl.BlockSpec((1, GLA_V_DIM), lambda bi, h, r: (0, h)),
        ],
        out_specs=pl.BlockSpec((1, rows, GLA_V_DIM), lambda bi, h, r: (bi, r, h)),
        out_shape=jax.ShapeDtypeStruct((b, s, GLA_V_WIDTH), BF16),
        scratch_shapes=[pltpu.VMEM((GLA_V_DIM, GLA_K_DIM), F32)],
        compiler_params=_params("parallel", "parallel", "arbitrary"),
        name="gla",
    )(proj, proj, proj, proj, log_a, g.reshape(1, GLA_V_WIDTH))


def _mix_out_body(a_ref, b_ref, wa_ref, wb_ref, x_ref, o_ref):
    o_ref[...] = x_ref[...] + _dot(a_ref[...], wa_ref[...]) + _dot(b_ref[...], wb_ref[...])


def _mix_out(sb_o, gla_o, w_out, x, *, tm, tn):
    m, n = x.shape
    ka, kb = sb_o.shape[1], gla_o.shape[1]
    assert ka == kb
    return pl.pallas_call(
        _mix_out_body,
        grid=(m // tm, n // tn),
        in_specs=[
            pl.BlockSpec((tm, ka), lambda i, j: (i, 0)),
            pl.BlockSpec((tm, kb), lambda i, j: (i, 0)),
            pl.BlockSpec((ka, tn), lambda i, j: (0, j)),
            pl.BlockSpec((kb, tn), lambda i, j: (1, j)),
            pl.BlockSpec((tm, tn), lambda i, j: (i, j)),
        ],
        out_specs=pl.BlockSpec((tm, tn), lambda i, j: (i, j)),
        out_shape=jax.ShapeDtypeStruct((m, n), F32),
        compiler_params=_params("parallel", "arbitrary"),
        name="mix_out",
    )(sb_o, gla_o, w_out, w_out, x)


def _cross_body(q_ref, kv_ref, w_ref, x_ref, o_ref, a_scr):
    @pl.when(pl.program_id(1) == 0)
    def _():
        for h in range(CROSS_HEADS):
            lo, hi = h * CROSS_HEAD_DIM, (h + 1) * CROSS_HEAD_DIM
            q = q_ref[:, lo:hi]
            k = kv_ref[0, :, lo:hi]
            v = kv_ref[0, :, D_MODEL + lo:D_MODEL + hi]
            s = lax.dot_general(q, k, _NT, preferred_element_type=F32) * (CROSS_HEAD_DIM ** -0.5)
            e = jnp.exp(s - jnp.max(s, axis=-1, keepdims=True))
            p = e / jnp.sum(e, axis=-1, keepdims=True)
            a_scr[:, lo:hi] = _dot(p.astype(BF16), v).astype(BF16)

    o_ref[...] = x_ref[...] + _dot(a_scr[...], w_ref[...])


def _cross_attention(q, kv, w_co, x, *, seq, tm, tn):
    m, n = x.shape
    mem_len = kv.shape[1]
    tiles_per_batch = seq // tm
    return pl.pallas_call(
        _cross_body,
        grid=(m // tm, n // tn),
        in_specs=[
            pl.BlockSpec((tm, n), lambda i, j: (i, 0)),
            pl.BlockSpec((1, mem_len, 2 * n), lambda i, j: (i // tiles_per_batch, 0, 0)),
            pl.BlockSpec((n, tn), lambda i, j: (0, j)),
            pl.BlockSpec((tm, tn), lambda i, j: (i, j)),
        ],
        out_specs=pl.BlockSpec((tm, tn), lambda i, j: (i, j)),
        out_shape=jax.ShapeDtypeStruct((m, n), F32),
        scratch_shapes=[pltpu.VMEM((tm, n), BF16)],
        compiler_params=_params("parallel", "arbitrary"),
        name="cross_attention",
    )(q, kv, w_co, x)


def _router_body(x_ref, g_ref, w_ref, b_ref, h_ref, idx_ref, wt_ref):
    h = _rms(x_ref[...], g_ref[...])
    h_hi, h_lo = _split_bf16(h)
    h_ref[...] = _pack_halves(h_hi)
    logits = _dot(h_hi, w_ref[0]) + _dot(h_lo, w_ref[0]) + _dot(h_hi, w_ref[1]) + b_ref[...]
    lane = lax.broadcasted_iota(I32, logits.shape, 1)
    lane_f = lane.astype(F32)
    idx_out = jnp.zeros(logits.shape, F32)
    tops = []
    for k in range(TOP_K):
        m = jnp.max(logits, axis=-1, keepdims=True)
        pick = jnp.min(jnp.where(logits == m, lane_f, float(LANES)), axis=-1, keepdims=True)
        idx_out = jnp.where(lane == k, pick, idx_out)
        tops.append(m)
        logits = jnp.where(lane_f == pick, -jnp.inf, logits)
    es = [jnp.exp(t - tops[0]) for t in tops]
    inv = 1.0 / functools.reduce(lambda a, c: a + c, es)
    wt = jnp.zeros(logits.shape, F32)
    for k in range(TOP_K):
        wt = jnp.where(lane == k, es[k] * inv, wt)
    idx_ref[...] = idx_out.astype(I32)
    wt_ref[...] = wt


def _router(x, g, w_split, b_pad, *, tm):
    m, k = x.shape
    return pl.pallas_call(
        _router_body,
        grid=(m // tm,),
        in_specs=[
            pl.BlockSpec((tm, k), lambda i: (i, 0)),
            pl.BlockSpec((1, k), lambda i: (0, 0)),
            pl.BlockSpec((2, k, LANES), lambda i: (0, 0, 0)),
            pl.BlockSpec((1, LANES), lambda i: (0, 0)),
        ],
        out_specs=[
            pl.BlockSpec((tm, k // 2), lambda i: (i, 0)),
            pl.BlockSpec((tm, LANES), lambda i: (i, 0)),
            pl.BlockSpec((tm, LANES), lambda i: (i, 0)),
        ],
        out_shape=[
            jax.ShapeDtypeStruct((m, k // 2), U32),
            jax.ShapeDtypeStruct((m, LANES), I32),
            jax.ShapeDtypeStruct((m, LANES), F32),
        ],
        compiler_params=_params("parallel"),
        name="router",
    )(x, g.reshape(1, k), w_split, b_pad)


def _stage_indices(idx_hbm, idx_smem, sem):
    step = pl.program_id(0)
    n = idx_smem.shape[0]
    cp = pltpu.make_async_copy(idx_hbm.at[pl.ds(pl.multiple_of(step * n, n), n)], idx_smem, sem)
    cp.start()
    cp.wait()
    return step * n


def _copy_rows(n, src_of, dst_of, src_hbm, dst_hbm, sem):
    def row_copy(a):
        return pltpu.make_async_copy(src_hbm.at[pl.ds(src_of(a), 1)], dst_hbm.at[pl.ds(dst_of(a), 1)], sem)

    def start(a, c):
        row_copy(a).start()
        return c

    def wait(a, c):
        row_copy(a).wait()
        return c

    lax.fori_loop(0, n, start, 0, unroll=8)
    lax.fori_loop(0, n, wait, 0, unroll=8)


def _dispatch_body(dest_hbm, src_hbm, dst_hbm, idx_smem, sem_idx, sem):
    base = _stage_indices(dest_hbm, idx_smem, sem_idx)
    shift = TOP_K.bit_length() - 1
    _copy_rows(idx_smem.shape[0],
               lambda a: jnp.right_shift(base + a, shift),
               lambda a: idx_smem[a],
               src_hbm, dst_hbm, sem)


def _dispatch(dest, h, n_slots):
    n_assign = dest.shape[0]
    d = h.shape[1]
    return pl.pallas_call(
        _dispatch_body,
        grid_spec=pltpu.PrefetchScalarGridSpec(
            num_scalar_prefetch=0,
            grid=(n_assign // COPY_ROWS_PER_STEP,),
            in_specs=[pl.BlockSpec(memory_space=pl.ANY), pl.BlockSpec(memory_space=pl.ANY)],
            out_specs=pl.BlockSpec(memory_space=pl.ANY),
            scratch_shapes=[
                pltpu.SMEM((COPY_ROWS_PER_STEP,), I32),
                pltpu.SemaphoreType.DMA,
                pltpu.SemaphoreType.DMA,
            ],
        ),
        out_shape=jax.ShapeDtypeStruct((n_slots, d), h.dtype),
        compiler_params=_params("arbitrary"),
        name="moe_dispatch",
    )(dest, h)


def _collect_body(dest_hbm, src_hbm, dst_hbm, idx_smem, sem_idx, sem):
    base = _stage_indices(dest_hbm, idx_smem, sem_idx)
    _copy_rows(idx_smem.shape[0], lambda a: idx_smem[a], lambda a: base + a, src_hbm, dst_hbm, sem)


def _collect(dest, ys):
    n_assign = dest.shape[0]
    d = ys.shape[1]
    return pl.pallas_call(
        _collect_body,
        grid_spec=pltpu.PrefetchScalarGridSpec(
            num_scalar_prefetch=0,
            grid=(n_assign // COPY_ROWS_PER_STEP,),
            in_specs=[pl.BlockSpec(memory_space=pl.ANY), pl.BlockSpec(memory_space=pl.ANY)],
            out_specs=pl.BlockSpec(memory_space=pl.ANY),
            scratch_shapes=[
                pltpu.SMEM((COPY_ROWS_PER_STEP,), I32),
                pltpu.SemaphoreType.DMA,
                pltpu.SemaphoreType.DMA,
            ],
        ),
        out_shape=jax.ShapeDtypeStruct((n_assign, d), ys.dtype),
        compiler_params=_params("arbitrary"),
        name="moe_collect",
    )(dest, ys)


def _experts_body(blk_ref, exp_ref, rows_ref, x_ref, wg_ref, wl_ref, bg_ref, bl_ref, wd_ref, bd_ref, y_ref,
                  xs_scr):
    c = pl.program_id(0)
    j = pl.program_id(1)
    rows = rows_ref[c]
    n_sub = jnp.right_shift(rows + MOE_SUB - 1, MOE_SUB.bit_length() - 1)
    half = xs_scr.shape[1] // 2

    def sub_rows(r):
        return pl.ds(pl.multiple_of(r * MOE_SUB, MOE_SUB), MOE_SUB)

    @pl.when(jnp.logical_and(j == 0, rows > 0))
    def _():
        y_ref[...] = jnp.broadcast_to(bd_ref[0], y_ref.shape)

        def unpack(r, carry):
            row = r * MOE_SUB + lax.broadcasted_iota(I32, (MOE_SUB, half), 0)
            lo, hi = _unpack_halves(jnp.where(row < rows, x_ref[sub_rows(r), :], jnp.uint32(0)))
            xs_scr[sub_rows(r), :half] = lo
            xs_scr[sub_rows(r), half:] = hi
            return carry

        lax.fori_loop(0, n_sub, unpack, 0)

    wg = wg_ref[0].astype(BF16)
    wl = wl_ref[0].astype(BF16)
    wd = wd_ref[0].astype(BF16)
    bg = bg_ref[0]
    bl = bl_ref[0]

    def sub(r, carry):
        xs = xs_scr[sub_rows(r), :]
        gate = jnp.minimum(_dot(xs, wg) + bg, SWIGLU_LIMIT)
        lin = jnp.clip(_dot(xs, wl) + bl, -SWIGLU_LIMIT, SWIGLU_LIMIT)
        act = gate * jax.nn.sigmoid(SWIGLU_ALPHA * gate) * (lin + 1.0)
        y_ref[sub_rows(r), :] += _dot(act.astype(BF16), wd)
        return carry

    lax.fori_loop(0, n_sub, sub, 0)


def _experts(chunk_blk, chunk_expert, chunk_rows, x_sorted, w_gate_up, b_gate_up, w_down, b_down):
    n_chunks = chunk_blk.shape[0]
    d = 2 * x_sorted.shape[1]
    n_j = D_FF // MOE_TN
    return pl.pallas_call(
        _experts_body,
        grid_spec=pltpu.PrefetchScalarGridSpec(
            num_scalar_prefetch=3,
            grid=(n_chunks, n_j),
            in_specs=[
                pl.BlockSpec((MOE_CHUNK, d // 2), lambda c, j, blk, ex, rw: (blk[c], 0)),
                pl.BlockSpec((1, d, MOE_TN), lambda c, j, blk, ex, rw: (ex[c], 0, j)),
                pl.BlockSpec((1, d, MOE_TN), lambda c, j, blk, ex, rw: (ex[c], 0, n_j + j)),
                pl.BlockSpec((1, 1, MOE_TN), lambda c, j, blk, ex, rw: (ex[c], 0, j)),
                pl.BlockSpec((1, 1, MOE_TN), lambda c, j, blk, ex, rw: (ex[c], 0, n_j + j)),
                pl.BlockSpec((1, MOE_TN, d), lambda c, j, blk, ex, rw: (ex[c], j, 0)),
                pl.BlockSpec((1, 1, d), lambda c, j, blk, ex, rw: (ex[c], 0, 0)),
            ],
            out_specs=pl.BlockSpec((MOE_CHUNK, d), lambda c, j, blk, ex, rw: (blk[c], 0)),
            scratch_shapes=[pltpu.VMEM((MOE_CHUNK, d), BF16)],
        ),
        out_shape=jax.ShapeDtypeStruct((n_chunks * MOE_CHUNK, d), F32),
        compiler_params=_params("arbitrary", "arbitrary"),
        name="moe_experts",
    )(chunk_blk, chunk_expert, chunk_rows, x_sorted, w_gate_up, w_gate_up,
      b_gate_up.reshape(N_EXPERTS, 1, -1), b_gate_up.reshape(N_EXPERTS, 1, -1),
      w_down, b_down.reshape(N_EXPERTS, 1, -1))


def _combine_body(x_ref, y_ref, wt_ref, g_ref, o_ref, *, final_norm):
    d = x_ref.shape[1]
    wt = wt_ref[...]
    acc = x_ref[...]
    for k in range(TOP_K):
        acc = acc + y_ref[:, k * d:(k + 1) * d] * wt[:, k:k + 1]
    o_ref[...] = _rms(acc, g_ref[...]) if final_norm else acc


def _combine(x, y_rows, wt, g, *, tm, final_norm):
    m, d = x.shape
    return pl.pallas_call(
        functools.partial(_combine_body, final_norm=final_norm),
        grid=(m // tm,),
        in_specs=[
            pl.BlockSpec((tm, d), lambda i: (i, 0)),
            pl.BlockSpec((tm, TOP_K * d), lambda i: (i, 0)),
            pl.BlockSpec((tm, LANES), lambda i: (i, 0)),
            pl.BlockSpec((1, d), lambda i: (0, 0)),
        ],
        out_specs=pl.BlockSpec((tm, d), lambda i: (i, 0)),
        out_shape=jax.ShapeDtypeStruct((m, d), F32),
        compiler_params=_params("parallel"),
        name="moe_combine",
    )(x, y_rows, wt, g.reshape(1, d))


def _routing_tables(top_idx, n_chunks):
    expert_flat = top_idx.reshape(-1)
    onehot = (expert_flat[:, None] == jnp.arange(N_EXPERTS, dtype=I32)[None, :]).astype(I32)
    running = jnp.cumsum(onehot, axis=0)
    rank = jnp.sum(onehot * running, axis=1) - 1
    counts = running[-1]
    chunks_per_expert = (counts + MOE_CHUNK - 1) // MOE_CHUNK
    chunk_end = jnp.cumsum(chunks_per_expert)
    chunk_start = chunk_end - chunks_per_expert
    dest = (chunk_start * MOE_CHUNK)[expert_flat] + rank

    c = jnp.arange(n_chunks, dtype=I32)
    used = chunk_end[-1]
    c_eff = jnp.minimum(c, used - 1)
    chunk_expert = jnp.minimum(jnp.searchsorted(chunk_end, c_eff, side="right"), N_EXPERTS - 1).astype(I32)
    rows = jnp.clip(counts[chunk_expert] - (c_eff - chunk_start[chunk_expert]) * MOE_CHUNK, 0, MOE_CHUNK)
    chunk_rows = jnp.where(c < used, rows, 0).astype(I32)
    return dest.astype(I32), c_eff.astype(I32), chunk_expert, chunk_rows


def kernel(x, mem, norm_mix_g, w_in, w_gla_gate_up, b_gla_gate, sb_norm_g, gla_norm_g, w_out,
           norm_cross_g, norm_mem_g, w_cq, w_ckv, w_co,
           norm_moe_g, w_router, b_router, w_gate_up, b_gate_up, w_down, b_down,
           norm_final_g):
    batch, seq, d = x.shape
    n_tok = batch * seq
    depth = w_in.shape[0]
    xf = x.reshape(n_tok, d)
    mem_f = mem.reshape(-1, d)
    mem_len = mem.shape[1]

    for layer in range(depth):
        w_in_l = w_in[layer]
        w_main = w_in_l[:, :PROJ_MAIN].astype(BF16)
        w_low = jnp.pad(w_in_l[:, PROJ_MAIN:], ((0, 0), (0, LANES - GLA_GATE_RANK))).astype(BF16)
        w_up = jnp.pad(w_gla_gate_up[layer], ((0, LANES - GLA_GATE_RANK), (0, 0))).astype(BF16)
        proj, log_a = _inproj(xf, norm_mix_g[layer], w_main, w_low, w_up, b_gla_gate[layer], tm=1024, tn=1024)
        proj = proj.reshape(batch, seq, PROJ_MAIN)
        sb_o = _sb_attention(proj, sb_norm_g[layer], tq=256)
        gla_o = _gla(proj, log_a.reshape(batch, seq, GLA_K_WIDTH), gla_norm_g[layer], rows=256)
        xf = _mix_out(sb_o.reshape(n_tok, SB_WIDTH), gla_o.reshape(n_tok, GLA_V_WIDTH),
                      w_out[layer].astype(BF16), xf, tm=1024, tn=1024)

        q = _norm_matmul(xf, norm_cross_g[layer], w_cq[layer].astype(BF16), tm=1024, tn=1024, out_dtype=BF16)
        kv = _norm_matmul(mem_f, norm_mem_g[layer], w_ckv[layer].astype(BF16), tm=1024, tn=1024, out_dtype=BF16)
        xf = _cross_attention(q, kv.reshape(batch, mem_len, 2 * d), w_co[layer].astype(BF16), xf,
                              seq=seq, tm=1024, tn=1024)

        w_r = jnp.pad(w_router[layer], ((0, 0), (0, LANES - N_EXPERTS)))
        w_r_hi = w_r.astype(BF16)
        w_r_lo = (w_r - w_r_hi.astype(F32)).astype(BF16)
        b_r = jnp.pad(b_router[layer], (0, LANES - N_EXPERTS), constant_values=-1e30).reshape(1, LANES)
        h, top_idx, top_w = _router(xf, norm_moe_g[layer], jnp.stack([w_r_hi, w_r_lo]), b_r, tm=512)

        n_assign = n_tok * TOP_K
        n_chunks = n_assign // MOE_CHUNK + N_EXPERTS
        dest, chunk_blk, chunk_expert, chunk_rows = _routing_tables(top_idx[:, :TOP_K], n_chunks)
        x_sorted = _dispatch(dest, h, n_chunks * MOE_CHUNK)
        ys = _experts(chunk_blk, chunk_expert, chunk_rows, x_sorted,
                      w_gate_up[layer], b_gate_up[layer], w_down[layer], b_down[layer])
        y_rows = _collect(dest, ys).reshape(n_tok, TOP_K * d)
        xf = _combine(xf, y_rows, top_w, norm_final_g, tm=256, final_norm=layer + 1 == depth)

    return xf.reshape(batch, seq, d)
```

```python
import functools

import jax
import jax.numpy as jnp
from jax import lax
from jax.experimental import pallas as pl
from jax.experimental.pallas import tpu as pltpu

F32 = jnp.float32
BF16 = jnp.bfloat16
I32 = jnp.int32

D_MODEL = 2048
SB_HEADS = 8
SB_HEAD_DIM = 128
SB_WIDTH = SB_HEADS * SB_HEAD_DIM
GLA_HEADS = 4
GLA_K_DIM = 128
GLA_V_DIM = 256
GLA_K_WIDTH = GLA_HEADS * GLA_K_DIM
GLA_V_WIDTH = GLA_HEADS * GLA_V_DIM
GLA_GATE_RANK = 16
GLA_GATE_TAU = 16.0
GLA_CHUNK = 64
PROJ_MAIN = 3 * SB_WIDTH + 2 * GLA_K_WIDTH + 2 * GLA_V_WIDTH
CROSS_HEADS = 4
CROSS_HEAD_DIM = D_MODEL // CROSS_HEADS
N_EXPERTS = 32
TOP_K = 4
D_FF = D_MODEL
SWIGLU_LIMIT = 7.0
SWIGLU_ALPHA = 1.702
EPS = 1e-6

LANES = 128
VMEM_LIMIT_BYTES = 56 * 1024 * 1024

_SBQ_BLK = 0
_SBK_BLK = SB_WIDTH // LANES
_SBV_BLK = 2 * SB_WIDTH // LANES
_GQ_BLK = 3 * SB_WIDTH // LANES
_GK_BLK = _GQ_BLK + GLA_K_WIDTH // LANES
_GV_BLK256 = (3 * SB_WIDTH + 2 * GLA_K_WIDTH) // GLA_V_DIM
_GR_BLK256 = _GV_BLK256 + GLA_V_WIDTH // GLA_V_DIM

SB_ZERO_LOG = -104.0

MOE_CHUNK = 1280
MOE_SUB = 256
MOE_TN = 256

_NT = (((1,), (1,)), ((), ()))
_TN = (((0,), (0,)), ((), ()))


def _params(*sem):
    return pltpu.CompilerParams(dimension_semantics=sem, vmem_limit_bytes=VMEM_LIMIT_BYTES)


def _rms(x, g):
    return x * lax.rsqrt(jnp.mean(x * x, axis=-1, keepdims=True) + EPS) * g


def _log_sigmoid(x):
    return -(jnp.maximum(-x, 0.0) + jnp.log1p(jnp.exp(-jnp.abs(x))))


def _split_bf16(x):
    hi = x.astype(BF16)
    lo = (x - hi.astype(F32)).astype(BF16)
    return hi, lo


def _dot(a, b):
    return jnp.dot(a, b, preferred_element_type=F32)


def _norm_matmul_body(x_ref, g_ref, w_ref, o_ref, h_scr):
    @pl.when(pl.program_id(1) == 0)
    def _():
        h_scr[...] = _rms(x_ref[...], g_ref[...]).astype(BF16)

    o_ref[...] = _dot(h_scr[...], w_ref[...]).astype(o_ref.dtype)


def _norm_matmul(x, g, w, *, tm, tn, out_dtype):
    m, k = x.shape
    n = w.shape[1]
    return pl.pallas_call(
        _norm_matmul_body,
        grid=(m // tm, n // tn),
        in_specs=[
            pl.BlockSpec((tm, k), lambda i, j: (i, 0)),
            pl.BlockSpec((1, k), lambda i, j: (0, 0)),
            pl.BlockSpec((k, tn), lambda i, j: (0, j)),
        ],
        out_specs=pl.BlockSpec((tm, tn), lambda i, j: (i, j)),
        out_shape=jax.ShapeDtypeStruct((m, n), out_dtype),
        scratch_shapes=[pltpu.VMEM((tm, k), BF16)],
        compiler_params=_params("parallel", "arbitrary"),
        name="norm_matmul",
    )(x, g.reshape(1, k), w)


def _inproj_body(x_ref, g_ref, w_ref, wlow_ref, wup_ref, bup_ref, o_ref, la_ref, h_scr):
    @pl.when(pl.program_id(1) == 0)
    def _():
        h = _rms(x_ref[...], g_ref[...]).astype(BF16)
        h_scr[...] = h
        low = _dot(h, wlow_ref[...])
        pre = _dot(low.astype(BF16), wup_ref[...]) + bup_ref[...]
        la_ref[...] = _log_sigmoid(pre) / GLA_GATE_TAU

    o_ref[...] = _dot(h_scr[...], w_ref[...]).astype(o_ref.dtype)


def _inproj(x, g, w_main, w_low, w_up, b_up, *, tm, tn):
    m, k = x.shape
    n = w_main.shape[1]
    return pl.pallas_call(
        _inproj_body,
        grid=(m // tm, n // tn),
        in_specs=[
            pl.BlockSpec((tm, k), lambda i, j: (i, 0)),
            pl.BlockSpec((1, k), lambda i, j: (0, 0)),
            pl.BlockSpec((k, tn), lambda i, j: (0, j)),
            pl.BlockSpec((k, LANES), lambda i, j: (0, 0)),
            pl.BlockSpec((LANES, GLA_K_WIDTH), lambda i, j: (0, 0)),
            pl.BlockSpec((1, GLA_K_WIDTH), lambda i, j: (0, 0)),
        ],
        out_specs=[
            pl.BlockSpec((tm, tn), lambda i, j: (i, j)),
            pl.BlockSpec((tm, GLA_K_WIDTH), lambda i, j: (i, 0)),
        ],
        out_shape=[
            jax.ShapeDtypeStruct((m, n), BF16),
            jax.ShapeDtypeStruct((m, GLA_K_WIDTH), F32),
        ],
        scratch_shapes=[pltpu.VMEM((tm, k), BF16)],
        compiler_params=_params("parallel", "arbitrary"),
        name="inproj",
    )(x, g.reshape(1, k), w_main, w_low, w_up, b_up.reshape(1, -1))


def _sb_body(q_ref, k_ref, v_ref, g_ref, o_ref, acc_ref, carry_ref, *, tq):
    qi = pl.program_id(2)
    q = q_ref[0]
    scale = SB_HEAD_DIM ** -0.5
    row = lax.broadcasted_iota(I32, (tq, tq), 0)
    col = lax.broadcasted_iota(I32, (tq, tq), 1)
    causal = col < row
    later = (row > col).astype(BF16)

    def tile(kj, diag):
        off = pl.multiple_of(kj * tq, tq)
        kblk = k_ref[0, pl.ds(off, tq), :]
        vblk = v_ref[0, pl.ds(off, tq), :]
        z = lax.dot_general(q, kblk, _NT, preferred_element_type=F32) * scale
        lp = jnp.log1p(jnp.exp(-jnp.abs(z)))
        log_beta = jnp.minimum(z, 0.0) - lp
        log_1m = log_beta - z
        if diag:
            log_1m = jnp.where(causal, log_1m, 0.0)
        hi, lo = _split_bf16(log_1m)
        between = _dot(hi, later) + _dot(lo, later)
        carry = carry_ref[...]
        a = jnp.exp(log_beta + between + carry)
        if diag:
            a = jnp.where(causal, a, 0.0)
        acc_ref[...] += _dot(a.astype(BF16), vblk)
        carry = carry + jnp.sum(log_1m, axis=-1, keepdims=True)
        carry_ref[...] = carry
        return jnp.max(carry)

    acc_ref[...] = jnp.zeros_like(acc_ref)
    carry_ref[...] = jnp.zeros_like(carry_ref)
    top = tile(qi, True)

    def cond(state):
        kj, top = state
        return jnp.logical_and(kj >= 0, top > SB_ZERO_LOG)

    def body(state):
        kj, _ = state
        return kj - 1, tile(kj, False)

    lax.while_loop(cond, body, (qi - 1, top))
    o_ref[0] = _rms(acc_ref[...], g_ref[...]).astype(o_ref.dtype)


def _sb_attention(proj, g, *, tq):
    b, s, _ = proj.shape
    return pl.pallas_call(
        functools.partial(_sb_body, tq=tq),
        grid=(b, SB_HEADS, s // tq),
        in_specs=[
            pl.BlockSpec((1, tq, SB_HEAD_DIM), lambda bi, h, qi: (bi, qi, _SBQ_BLK + h)),
            pl.BlockSpec((1, s, SB_HEAD_DIM), lambda bi, h, qi: (bi, 0, _SBK_BLK + h)),
            pl.BlockSpec((1, s, SB_HEAD_DIM), lambda bi, h, qi: (bi, 0, _SBV_BLK + h)),
            pl.BlockSpec((1, SB_HEAD_DIM), lambda bi, h, qi: (0, h)),
        ],
        out_specs=pl.BlockSpec((1, tq, SB_HEAD_DIM), lambda bi, h, qi: (bi, qi, h)),
        out_shape=jax.ShapeDtypeStruct((b, s, SB_WIDTH), BF16),
        scratch_shapes=[pltpu.VMEM((tq, SB_HEAD_DIM), F32), pltpu.VMEM((tq, 1), F32)],
        compiler_params=_params("parallel", "parallel", "arbitrary"),
        name="sb_attention",
    )(proj, proj, proj, g.reshape(1, SB_WIDTH))


def _gla_body(q_ref, k_ref, v_ref, r_ref, la_ref, g_ref, o_ref, st_ref, *, rows):
    @pl.when(pl.program_id(2) == 0)
    def _():
        st_ref[...] = jnp.zeros_like(st_ref)

    shift = GLA_CHUNK.bit_length() - 1
    ri = lax.broadcasted_iota(I32, (rows, rows), 0)
    ci = lax.broadcasted_iota(I32, (rows, rows), 1)
    same = jnp.right_shift(ri, shift) == jnp.right_shift(ci, shift)
    tril = jnp.logical_and(same, ci <= ri)
    hi, lo = _split_bf16(la_ref[0])
    tril_b = tril.astype(BF16)
    same_b = same.astype(BF16)
    gcum = _dot(tril_b, hi) + _dot(tril_b, lo)
    glast = _dot(same_b, hi) + _dot(same_b, lo)

    q = q_ref[0].astype(F32) * (GLA_K_DIM ** -0.5)
    k = k_ref[0].astype(F32)
    v = v_ref[0]
    q_in = (q * jnp.exp(gcum)).astype(BF16)
    k_in = (k * jnp.exp(-gcum)).astype(BF16)
    k_out = (k * jnp.exp(glast - gcum)).astype(BF16)
    scores = lax.dot_general(q_in, k_in, _NT, preferred_element_type=F32)
    scores = jnp.where(tril, scores, 0.0)
    o_intra = _dot(scores.astype(BF16), v)

    state_t = st_ref[...]
    outs = []
    for n in range(rows // GLA_CHUNK):
        sl = slice(n * GLA_CHUNK, (n + 1) * GLA_CHUNK)
        o_inter = lax.dot_general(q_in[sl], state_t.astype(BF16), _NT, preferred_element_type=F32)
        outs.append(o_intra[sl] + o_inter)
        kv_t = lax.dot_general(v[sl], k_out[sl], _TN, preferred_element_type=F32)
        decay = jnp.exp(glast[n * GLA_CHUNK:n * GLA_CHUNK + 1, :])
        state_t = state_t * decay + kv_t
    st_ref[...] = state_t

    o = jnp.concatenate(outs, axis=0)
    gate = r_ref[0].astype(F32)
    y = _rms(o, g_ref[...]) * (gate * jax.nn.sigmoid(gate))
    o_ref[0] = y.astype(o_ref.dtype)


def _gla(proj, log_a, g, *, rows):
    b, s, _ = proj.shape
    return pl.pallas_call(
        functools.partial(_gla_body, rows=rows),
        grid=(b, GLA_HEADS, s // rows),
        in_specs=[
            pl.BlockSpec((1, rows, GLA_K_DIM), lambda bi, h, r: (bi, r, _GQ_BLK + h)),
            pl.BlockSpec((1, rows, GLA_K_DIM), lambda bi, h, r: (bi, r, _GK_BLK + h)),
            pl.BlockSpec((1, rows, GLA_V_DIM), lambda bi, h, r: (bi, r, _GV_BLK256 + h)),
            pl.BlockSpec((1, rows, GLA_V_DIM), lambda bi, h, r: (bi, r, _GR_BLK256 + h)),
            pl.BlockSpec((1, rows, GLA_K_DIM), lambda bi, h, r: (bi, r, h)),
            pl.BlockSpec((1, GLA_V_DIM), lambda bi, h, r: (0, h)),
        ],
        out_specs=pl.BlockSpec((1, rows, GLA_V_DIM), lambda bi, h, r: (bi, r, h)),
        out_shape=jax.ShapeDtypeStruct((b, s, GLA_V_WIDTH), BF16),
        scratch_shapes=[pltpu.VMEM((GLA_V_DIM, GLA_K_DIM), F32)],
        compiler_params=_params("parallel", "parallel", "arbitrary"),
        name="gla",
    )(proj, proj, proj, proj, log_a, g.reshape(1, GLA_V_WIDTH))


def _mix_out_body(a_ref, b_ref, wa_ref, wb_ref, x_ref, o_ref):
    o_ref[...] = x_ref[...] + _dot(a_ref[...], wa_ref[...]) + _dot(b_ref[...], wb_ref[...])


def _mix_out(sb_o, gla_o, w_out, x, *, tm, tn):
    m, n = x.shape
    ka, kb = sb_o.shape[1], gla_o.shape[1]
    assert ka == kb
    return pl.pallas_call(
        _mix_out_body,
        grid=(m // tm, n // tn),
        in_specs=[
            pl.BlockSpec((tm, ka), lambda i, j: (i, 0)),
            pl.BlockSpec((tm, kb), lambda i, j: (i, 0)),
            pl.BlockSpec((ka, tn), lambda i, j: (0, j)),
            pl.BlockSpec((kb, tn), lambda i, j: (1, j)),
            pl.BlockSpec((tm, tn), lambda i, j: (i, j)),
        ],
        out_specs=pl.BlockSpec((tm, tn), lambda i, j: (i, j)),
        out_shape=jax.ShapeDtypeStruct((m, n), F32),
        compiler_params=_params("parallel", "arbitrary"),
        name="mix_out",
    )(sb_o, gla_o, w_out, w_out, x)


def _cross_body(q_ref, kv_ref, w_ref, x_ref, o_ref, a_scr):
    @pl.when(pl.program_id(1) == 0)
    def _():
        for h in range(CROSS_HEADS):
            lo, hi = h * CROSS_HEAD_DIM, (h + 1) * CROSS_HEAD_DIM
            q = q_ref[:, lo:hi]
            k = kv_ref[0, :, lo:hi]
            v = kv_ref[0, :, D_MODEL + lo:D_MODEL + hi]
            s = lax.dot_general(q, k, _NT, preferred_element_type=F32) * (CROSS_HEAD_DIM ** -0.5)
            e = jnp.exp(s - jnp.max(s, axis=-1, keepdims=True))
            p = e / jnp.sum(e, axis=-1, keepdims=True)
            a_scr[:, lo:hi] = _dot(p.astype(BF16), v).astype(BF16)

    o_ref[...] = x_ref[...] + _dot(a_scr[...], w_ref[...])


def _cross_attention(q, kv, w_co, x, *, seq, tm, tn):
    m, n = x.shape
    mem_len = kv.shape[1]
    tiles_per_batch = seq // tm
    return pl.pallas_call(
        _cross_body,
        grid=(m // tm, n // tn),
        in_specs=[
            pl.BlockSpec((tm, n), lambda i, j: (i, 0)),
            pl.BlockSpec((1, mem_len, 2 * n), lambda i, j: (i // tiles_per_batch, 0, 0)),
            pl.BlockSpec((n, tn), lambda i, j: (0, j)),
            pl.BlockSpec((tm, tn), lambda i, j: (i, j)),
        ],
        out_specs=pl.BlockSpec((tm, tn), lambda i, j: (i, j)),
        out_shape=jax.ShapeDtypeStruct((m, n), F32),
        scratch_shapes=[pltpu.VMEM((tm, n), BF16)],
        compiler_params=_params("parallel", "arbitrary"),
        name="cross_attention",
    )(q, kv, w_co, x)


def _router_body(x_ref, g_ref, w_ref, b_ref, idx_ref, wt_ref):
    h_hi, h_lo = _split_bf16(_rms(x_ref[...], g_ref[...]))
    logits = _dot(h_hi, w_ref[0]) + _dot(h_lo, w_ref[0]) + _dot(h_hi, w_ref[1]) + b_ref[...]
    lane = lax.broadcasted_iota(I32, logits.shape, 1)
    lane_f = lane.astype(F32)
    idx_out = jnp.zeros(logits.shape, F32)
    tops = []
    for k in range(TOP_K):
        m = jnp.max(logits, axis=-1, keepdims=True)
        pick = jnp.min(jnp.where(logits == m, lane_f, float(LANES)), axis=-1, keepdims=True)
        idx_out = jnp.where(lane == k, pick, idx_out)
        tops.append(m)
        logits = jnp.where(lane_f == pick, -jnp.inf, logits)
    es = [jnp.exp(t - tops[0]) for t in tops]
    inv = 1.0 / functools.reduce(lambda a, c: a + c, es)
    wt = jnp.zeros(logits.shape, F32)
    for k in range(TOP_K):
        wt = jnp.where(lane == k, es[k] * inv, wt)
    idx_ref[...] = idx_out.astype(I32)
    wt_ref[...] = wt


def _router(x, g, w_split, b_pad, *, tm):
    m, k = x.shape
    return pl.pallas_call(
        _router_body,
        grid=(m // tm,),
        in_specs=[
            pl.BlockSpec((tm, k), lambda i: (i, 0)),
            pl.BlockSpec((1, k), lambda i: (0, 0)),
            pl.BlockSpec((2, k, LANES), lambda i: (0, 0, 0)),
            pl.BlockSpec((1, LANES), lambda i: (0, 0)),
        ],
        out_specs=[
            pl.BlockSpec((tm, LANES), lambda i: (i, 0)),
            pl.BlockSpec((tm, LANES), lambda i: (i, 0)),
        ],
        out_shape=[
            jax.ShapeDtypeStruct((m, LANES), I32),
            jax.ShapeDtypeStruct((m, LANES), F32),
        ],
        compiler_params=_params("parallel"),
        name="router",
    )(x, g.reshape(1, k), w_split, b_pad)


def _stage_indices(idx_hbm, idx_smem, sem):
    step = pl.program_id(0)
    n = idx_smem.shape[0]
    cp = pltpu.make_async_copy(idx_hbm.at[pl.ds(pl.multiple_of(step * n, n), n)], idx_smem, sem)
    cp.start()
    cp.wait()


def _for_each_assignment(n, row_copy):
    def start(a, c):
        row_copy(a).start()
        return c

    def wait(a, c):
        row_copy(a).wait()
        return c

    lax.fori_loop(0, n, start, 0, unroll=8)
    lax.fori_loop(0, n, wait, 0, unroll=8)


def _dispatch_body(dest_hbm, x_ref, g_ref, rows_hbm, idx_smem, stage, sem_idx, sem):
    _stage_indices(dest_hbm, idx_smem, sem_idx)
    h = _rms(x_ref[...], g_ref[...])
    for s in range(stage.shape[1]):
        stage[:, s, :] = h[:, s * LANES:(s + 1) * LANES]
    shift = TOP_K.bit_length() - 1

    def row_copy(a):
        return pltpu.make_async_copy(stage.at[pl.ds(jnp.right_shift(a, shift), 1)],
                                     rows_hbm.at[pl.ds(idx_smem[a], 1)], sem)

    _for_each_assignment(idx_smem.shape[0], row_copy)


def _dispatch(dest, x, g, n_slots, *, tm):
    m, d = x.shape
    return pl.pallas_call(
        _dispatch_body,
        grid_spec=pltpu.PrefetchScalarGridSpec(
            num_scalar_prefetch=0,
            grid=(m // tm,),
            in_specs=[
                pl.BlockSpec(memory_space=pl.ANY),
                pl.BlockSpec((tm, d), lambda i: (i, 0)),
                pl.BlockSpec((1, d), lambda i: (0, 0)),
            ],
            out_specs=pl.BlockSpec(memory_space=pl.ANY),
            scratch_shapes=[
                pltpu.SMEM((TOP_K * tm,), I32),
                pltpu.VMEM((tm, d // LANES, LANES), F32),
                pltpu.SemaphoreType.DMA,
                pltpu.SemaphoreType.DMA,
            ],
        ),
        out_shape=jax.ShapeDtypeStruct((n_slots, d // LANES, LANES), F32),
        compiler_params=_params("arbitrary"),
        name="moe_dispatch",
    )(dest, x, g.reshape(1, d))


def _experts_body(blk_ref, exp_ref, rows_ref, x_ref, wg_ref, wl_ref, bg_ref, bl_ref, wd_ref, bd_ref, y_ref,
                  xs_scr):
    c = pl.program_id(0)
    j = pl.program_id(1)
    rows = rows_ref[c]
    n_sub = jnp.right_shift(rows + MOE_SUB - 1, MOE_SUB.bit_length() - 1)
    n_seg = x_ref.shape[1]

    def sub_rows(r):
        return pl.ds(pl.multiple_of(r * MOE_SUB, MOE_SUB), MOE_SUB)

    @pl.when(jnp.logical_and(j == 0, rows > 0))
    def _():
        bias = bd_ref[0]
        for s in range(n_seg):
            y_ref[:, s, :] = jnp.broadcast_to(bias[:, s * LANES:(s + 1) * LANES], (y_ref.shape[0], LANES))

        def to_matrix(r, carry):
            live = r * MOE_SUB + lax.broadcasted_iota(I32, (MOE_SUB, LANES), 0) < rows
            for s in range(n_seg):
                seg = jnp.where(live, x_ref[sub_rows(r), s, :], 0.0)
                xs_scr[sub_rows(r), s * LANES:(s + 1) * LANES] = seg.astype(BF16)
            return carry

        lax.fori_loop(0, n_sub, to_matrix, 0)

    wg = wg_ref[0].astype(BF16)
    wl = wl_ref[0].astype(BF16)
    wd = wd_ref[0].astype(BF16)
    bg = bg_ref[0]
    bl = bl_ref[0]

    def sub(r, carry):
        xs = xs_scr[sub_rows(r), :]
        gate = jnp.minimum(_dot(xs, wg) + bg, SWIGLU_LIMIT)
        lin = jnp.clip(_dot(xs, wl) + bl, -SWIGLU_LIMIT, SWIGLU_LIMIT)
        act = gate * jax.nn.sigmoid(SWIGLU_ALPHA * gate) * (lin + 1.0)
        out = _dot(act.astype(BF16), wd)
        for s in range(n_seg):
            y_ref[sub_rows(r), s, :] += out[:, s * LANES:(s + 1) * LANES]
        return carry

    lax.fori_loop(0, n_sub, sub, 0)


def _experts(chunk_blk, chunk_expert, chunk_rows, x_rows, w_gate_up, b_gate_up, w_down, b_down):
    n_chunks = chunk_blk.shape[0]
    n_seg = x_rows.shape[1]
    d = n_seg * LANES
    n_j = D_FF // MOE_TN
    return pl.pallas_call(
        _experts_body,
        grid_spec=pltpu.PrefetchScalarGridSpec(
            num_scalar_prefetch=3,
            grid=(n_chunks, n_j),
            in_specs=[
                pl.BlockSpec((MOE_CHUNK, n_seg, LANES), lambda c, j, blk, ex, rw: (blk[c], 0, 0),
                             pipeline_mode=pl.Buffered(1)),
                pl.BlockSpec((1, d, MOE_TN), lambda c, j, blk, ex, rw: (ex[c], 0, j)),
                pl.BlockSpec((1, d, MOE_TN), lambda c, j, blk, ex, rw: (ex[c], 0, n_j + j)),
                pl.BlockSpec((1, 1, MOE_TN), lambda c, j, blk, ex, rw: (ex[c], 0, j)),
                pl.BlockSpec((1, 1, MOE_TN), lambda c, j, blk, ex, rw: (ex[c], 0, n_j + j)),
                pl.BlockSpec((1, MOE_TN, d), lambda c, j, blk, ex, rw: (ex[c], j, 0)),
                pl.BlockSpec((1, 1, d), lambda c, j, blk, ex, rw: (ex[c], 0, 0)),
            ],
            out_specs=pl.BlockSpec((MOE_CHUNK, n_seg, LANES), lambda c, j, blk, ex, rw: (blk[c], 0, 0)),
            scratch_shapes=[pltpu.VMEM((MOE_CHUNK, d), BF16)],
        ),
        out_shape=jax.ShapeDtypeStruct((n_chunks * MOE_CHUNK, n_seg, LANES), F32),
        compiler_params=_params("arbitrary", "arbitrary"),
        name="moe_experts",
    )(chunk_blk, chunk_expert, chunk_rows, x_rows, w_gate_up, w_gate_up,
      b_gate_up.reshape(N_EXPERTS, 1, -1), b_gate_up.reshape(N_EXPERTS, 1, -1),
      w_down, b_down.reshape(N_EXPERTS, 1, -1))


def _combine_body(dest_hbm, x_ref, wt_ref, g_ref, rows_hbm, o_ref, idx_smem, ybuf, sem_idx, sem, *, final_norm):
    _stage_indices(dest_hbm, idx_smem, sem_idx)
    shift = TOP_K.bit_length() - 1

    def row_copy(a):
        return pltpu.make_async_copy(rows_hbm.at[idx_smem[a]],
                                     ybuf.at[jnp.right_shift(a, shift), jnp.bitwise_and(a, TOP_K - 1)], sem)

    _for_each_assignment(idx_smem.shape[0], row_copy)

    wt = wt_ref[...]
    n_seg = ybuf.shape[2]
    segs = []
    for s in range(n_seg):
        acc = x_ref[:, s * LANES:(s + 1) * LANES]
        for k in range(TOP_K):
            acc = acc + ybuf[:, k, s, :] * wt[:, k:k + 1]
        segs.append(acc)
    if final_norm:
        ssq = functools.reduce(lambda a, c: a + c, [jnp.sum(v * v, axis=-1, keepdims=True) for v in segs])
        inv = lax.rsqrt(ssq / (n_seg * LANES) + EPS)
        segs = [v * inv * g_ref[:, s * LANES:(s + 1) * LANES] for s, v in enumerate(segs)]
    for s, v in enumerate(segs):
        o_ref[:, s * LANES:(s + 1) * LANES] = v


def _combine(dest, x, wt, g, y_rows, *, tm, final_norm):
    m, d = x.shape
    n_seg = d // LANES
    return pl.pallas_call(
        functools.partial(_combine_body, final_norm=final_norm),
        grid_spec=pltpu.PrefetchScalarGridSpec(
            num_scalar_prefetch=0,
            grid=(m // tm,),
            in_specs=[
                pl.BlockSpec(memory_space=pl.ANY),
                pl.BlockSpec((tm, d), lambda i: (i, 0)),
                pl.BlockSpec((tm, LANES), lambda i: (i, 0)),
                pl.BlockSpec((1, d), lambda i: (0, 0)),
                pl.BlockSpec(memory_space=pl.ANY),
            ],
            out_specs=pl.BlockSpec((tm, d), lambda i: (i, 0)),
            scratch_shapes=[
                pltpu.SMEM((TOP_K * tm,), I32),
                pltpu.VMEM((tm, TOP_K, n_seg, LANES), F32),
                pltpu.SemaphoreType.DMA,
                pltpu.SemaphoreType.DMA,
            ],
        ),
        out_shape=jax.ShapeDtypeStruct((m, d), F32),
        compiler_params=_params("arbitrary"),
        name="moe_combine",
    )(dest, x, wt, g.reshape(1, d), y_rows)


def _routing_tables(top_idx, n_chunks):
    expert_flat = top_idx.reshape(-1)
    onehot = (expert_flat[:, None] == jnp.arange(N_EXPERTS, dtype=I32)[None, :]).astype(I32)
    running = jnp.cumsum(onehot, axis=0)
    rank = jnp.sum(onehot * running, axis=1) - 1
    counts = running[-1]
    chunks_per_expert = (counts + MOE_CHUNK - 1) // MOE_CHUNK
    chunk_end = jnp.cumsum(chunks_per_expert)
    chunk_start = chunk_end - chunks_per_expert
    dest = (chunk_start * MOE_CHUNK)[expert_flat] + rank

    c = jnp.arange(n_chunks, dtype=I32)
    used = chunk_end[-1]
    c_eff = jnp.minimum(c, used - 1)
    chunk_expert = jnp.sum((chunk_end[None, :] <= c_eff[:, None]).astype(I32), axis=1)
    rows = jnp.clip(counts[chunk_expert] - (c_eff - chunk_start[chunk_expert]) * MOE_CHUNK, 0, MOE_CHUNK)
    chunk_rows = jnp.where(c < used, rows, 0).astype(I32)
    return dest.astype(I32), c_eff.astype(I32), chunk_expert, chunk_rows


def kernel(x, mem, norm_mix_g, w_in, w_gla_gate_up, b_gla_gate, sb_norm_g, gla_norm_g, w_out,
           norm_cross_g, norm_mem_g, w_cq, w_ckv, w_co,
           norm_moe_g, w_router, b_router, w_gate_up, b_gate_up, w_down, b_down,
           norm_final_g):
    batch, seq, d = x.shape
    n_tok = batch * seq
    depth = w_in.shape[0]
    xf = x.reshape(n_tok, d)
    mem_f = mem.reshape(-1, d)
    mem_len = mem.shape[1]

    for layer in range(depth):
        w_in_l = w_in[layer]
        w_main = w_in_l[:, :PROJ_MAIN].astype(BF16)
        w_low = jnp.pad(w_in_l[:, PROJ_MAIN:], ((0, 0), (0, LANES - GLA_GATE_RANK))).astype(BF16)
        w_up = jnp.pad(w_gla_gate_up[layer], ((0, LANES - GLA_GATE_RANK), (0, 0))).astype(BF16)
        proj, log_a = _inproj(xf, norm_mix_g[layer], w_main, w_low, w_up, b_gla_gate[layer], tm=1024, tn=1024)
        proj = proj.reshape(batch, seq, PROJ_MAIN)
        sb_o = _sb_attention(proj, sb_norm_g[layer], tq=256)
        gla_o = _gla(proj, log_a.reshape(batch, seq, GLA_K_WIDTH), gla_norm_g[layer], rows=256)
        xf = _mix_out(sb_o.reshape(n_tok, SB_WIDTH), gla_o.reshape(n_tok, GLA_V_WIDTH),
                      w_out[layer].astype(BF16), xf, tm=1024, tn=1024)

        q = _norm_matmul(xf, norm_cross_g[layer], w_cq[layer].astype(BF16), tm=1024, tn=1024, out_dtype=BF16)
        kv = _norm_matmul(mem_f, norm_mem_g[layer], w_ckv[layer].astype(BF16), tm=1024, tn=1024, out_dtype=BF16)
        xf = _cross_attention(q, kv.reshape(batch, mem_len, 2 * d), w_co[layer].astype(BF16), xf,
                              seq=seq, tm=1024, tn=1024)

        w_r = jnp.pad(w_router[layer], ((0, 0), (0, LANES - N_EXPERTS)))
        w_r_hi = w_r.astype(BF16)
        w_r_lo = (w_r - w_r_hi.astype(F32)).astype(BF16)
        b_r = jnp.pad(b_router[layer], (0, LANES - N_EXPERTS), constant_values=-1e30).reshape(1, LANES)
        top_idx, top_w = _router(xf, norm_moe_g[layer], jnp.stack([w_r_hi, w_r_lo]), b_r, tm=512)

        n_chunks = n_tok * TOP_K // MOE_CHUNK + N_EXPERTS
        dest, chunk_blk, chunk_expert, chunk_rows = _routing_tables(top_idx[:, :TOP_K], n_chunks)
        x_rows = _dispatch(dest, xf, norm_moe_g[layer], n_chunks * MOE_CHUNK, tm=512)
        y_rows = _experts(chunk_blk, chunk_expert, chunk_rows, x_rows,
                          w_gate_up[layer], b_gate_up[layer], w_down[layer], b_down[layer])
        xf = _combine(dest, xf, top_w, norm_final_g, y_rows, tm=256, final_norm=layer + 1 == depth)

    return xf.reshape(batch, seq, d)
```

```python
import functools

import jax
import jax.numpy as jnp
from jax import lax
from jax.experimental import pallas as pl
from jax.experimental.pallas import tpu as pltpu

F32 = jnp.float32
BF16 = jnp.bfloat16
I32 = jnp.int32

D_MODEL = 2048
SB_HEADS = 8
SB_HEAD_DIM = 128
SB_WIDTH = SB_HEADS * SB_HEAD_DIM
GLA_HEADS = 4
GLA_K_DIM = 128
GLA_V_DIM = 256
GLA_K_WIDTH = GLA_HEADS * GLA_K_DIM
GLA_V_WIDTH = GLA_HEADS * GLA_V_DIM
GLA_GATE_RANK = 16
GLA_GATE_TAU = 16.0
GLA_CHUNK = 64
PROJ_MAIN = 3 * SB_WIDTH + 2 * GLA_K_WIDTH + 2 * GLA_V_WIDTH
CROSS_HEADS = 4
CROSS_HEAD_DIM = D_MODEL // CROSS_HEADS
N_EXPERTS = 32
TOP_K = 4
D_FF = D_MODEL
SWIGLU_LIMIT = 7.0
SWIGLU_ALPHA = 1.702
EPS = 1e-6

LANES = 128
VMEM_BYTES = 64 * 1024 * 1024
VMEM_LIMIT_BYTES = VMEM_BYTES - 8 * 1024 * 1024

SEGS = D_MODEL // LANES
ROW_PITCH = 24

_SBQ_BLK = 0
_SBK_BLK = SB_WIDTH // LANES
_SBV_BLK = 2 * SB_WIDTH // LANES
_GQ_BLK = 3 * SB_WIDTH // LANES
_GK_BLK = _GQ_BLK + GLA_K_WIDTH // LANES
_GV_BLK256 = (3 * SB_WIDTH + 2 * GLA_K_WIDTH) // GLA_V_DIM
_GR_BLK256 = _GV_BLK256 + GLA_V_WIDTH // GLA_V_DIM

SB_ZERO_LOG = -104.0

MOE_CHUNK = 1280
MOE_SUB = 256
MOE_TN = 256

_NT = (((1,), (1,)), ((), ()))
_TN = (((0,), (0,)), ((), ()))


def _params(*sem):
    return pltpu.CompilerParams(dimension_semantics=sem, vmem_limit_bytes=VMEM_LIMIT_BYTES)


def _rms(x, g):
    return x * lax.rsqrt(jnp.mean(x * x, axis=-1, keepdims=True) + EPS) * g


def _log_sigmoid(x):
    return -(jnp.maximum(-x, 0.0) + jnp.log1p(jnp.exp(-jnp.abs(x))))


def _split_bf16(x):
    hi = x.astype(BF16)
    lo = (x - hi.astype(F32)).astype(BF16)
    return hi, lo


def _dot(a, b):
    return jnp.dot(a, b, preferred_element_type=F32)


def _norm_matmul_body(x_ref, g_ref, w_ref, o_ref, h_scr):
    @pl.when(pl.program_id(1) == 0)
    def _():
        h_scr[...] = _rms(x_ref[...], g_ref[...]).astype(BF16)

    o_ref[...] = _dot(h_scr[...], w_ref[...]).astype(o_ref.dtype)


def _norm_matmul(x, g, w, *, tm, tn, out_dtype):
    m, k = x.shape
    n = w.shape[1]
    return pl.pallas_call(
        _norm_matmul_body,
        grid=(m // tm, n // tn),
        in_specs=[
            pl.BlockSpec((tm, k), lambda i, j: (i, 0)),
            pl.BlockSpec((1, k), lambda i, j: (0, 0)),
            pl.BlockSpec((k, tn), lambda i, j: (0, j)),
        ],
        out_specs=pl.BlockSpec((tm, tn), lambda i, j: (i, j)),
        out_shape=jax.ShapeDtypeStruct((m, n), out_dtype),
        scratch_shapes=[pltpu.VMEM((tm, k), BF16)],
        compiler_params=_params("parallel", "arbitrary"),
        name="norm_matmul",
    )(x, g.reshape(1, k), w)


def _inproj_body(x_ref, g_ref, w_ref, wlow_ref, wup_ref, bup_ref, o_ref, la_ref, h_scr):
    @pl.when(pl.program_id(1) == 0)
    def _():
        h = _rms(x_ref[...], g_ref[...]).astype(BF16)
        h_scr[...] = h
        low = _dot(h, wlow_ref[...])
        pre = _dot(low.astype(BF16), wup_ref[...]) + bup_ref[...]
        la_ref[...] = _log_sigmoid(pre) / GLA_GATE_TAU

    o_ref[...] = _dot(h_scr[...], w_ref[...]).astype(o_ref.dtype)


def _inproj(x, g, w_main, w_low, w_up, b_up, *, tm, tn):
    m, k = x.shape
    n = w_main.shape[1]
    return pl.pallas_call(
        _inproj_body,
        grid=(m // tm, n // tn),
        in_specs=[
            pl.BlockSpec((tm, k), lambda i, j: (i, 0)),
            pl.BlockSpec((1, k), lambda i, j: (0, 0)),
            pl.BlockSpec((k, tn), lambda i, j: (0, j)),
            pl.BlockSpec((k, LANES), lambda i, j: (0, 0)),
            pl.BlockSpec((LANES, GLA_K_WIDTH), lambda i, j: (0, 0)),
            pl.BlockSpec((1, GLA_K_WIDTH), lambda i, j: (0, 0)),
        ],
        out_specs=[
            pl.BlockSpec((tm, tn), lambda i, j: (i, j)),
            pl.BlockSpec((tm, GLA_K_WIDTH), lambda i, j: (i, 0)),
        ],
        out_shape=[
            jax.ShapeDtypeStruct((m, n), BF16),
            jax.ShapeDtypeStruct((m, GLA_K_WIDTH), F32),
        ],
        scratch_shapes=[pltpu.VMEM((tm, k), BF16)],
        compiler_params=_params("parallel", "arbitrary"),
        name="inproj",
    )(x, g.reshape(1, k), w_main, w_low, w_up, b_up.reshape(1, -1))


def _sb_body(q_ref, k_ref, v_ref, g_ref, o_ref, acc_ref, carry_ref, *, tq):
    qi = pl.program_id(2)
    q = q_ref[0]
    scale = SB_HEAD_DIM ** -0.5
    row = lax.broadcasted_iota(I32, (tq, tq), 0)
    col = lax.broadcasted_iota(I32, (tq, tq), 1)
    causal = col < row
    later = (row > col).astype(BF16)

    def tile(kj, diag):
        off = pl.multiple_of(kj * tq, tq)
        kblk = k_ref[0, pl.ds(off, tq), :]
        vblk = v_ref[0, pl.ds(off, tq), :]
        z = lax.dot_general(q, kblk, _NT, preferred_element_type=F32) * scale
        lp = jnp.log1p(jnp.exp(-jnp.abs(z)))
        log_beta = jnp.minimum(z, 0.0) - lp
        log_1m = log_beta - z
        if diag:
            log_1m = jnp.where(causal, log_1m, 0.0)
        hi, lo = _split_bf16(log_1m)
        between = _dot(hi, later) + _dot(lo, later)
        carry = carry_ref[...]
        a = jnp.exp(log_beta + between + carry)
        if diag:
            a = jnp.where(causal, a, 0.0)
        acc_ref[...] += _dot(a.astype(BF16), vblk)
        carry = carry + jnp.sum(log_1m, axis=-1, keepdims=True)
        carry_ref[...] = carry
        return jnp.max(carry)

    acc_ref[...] = jnp.zeros_like(acc_ref)
    carry_ref[...] = jnp.zeros_like(carry_ref)
    top = tile(qi, True)

    def cond(state):
        kj, top = state
        return jnp.logical_and(kj >= 0, top > SB_ZERO_LOG)

    def body(state):
        kj, _ = state
        return kj - 1, tile(kj, False)

    lax.while_loop(cond, body, (qi - 1, top))
    o_ref[0] = _rms(acc_ref[...], g_ref[...]).astype(o_ref.dtype)


def _sb_attention(proj, g, *, tq):
    b, s, _ = proj.shape
    return pl.pallas_call(
        functools.partial(_sb_body, tq=tq),
        grid=(b, SB_HEADS, s // tq),
        in_specs=[
            pl.BlockSpec((1, tq, SB_HEAD_DIM), lambda bi, h, qi: (bi, qi, _SBQ_BLK + h)),
            pl.BlockSpec((1, s, SB_HEAD_DIM), lambda bi, h, qi: (bi, 0, _SBK_BLK + h)),
            pl.BlockSpec((1, s, SB_HEAD_DIM), lambda bi, h, qi: (bi, 0, _SBV_BLK + h)),
            pl.BlockSpec((1, SB_HEAD_DIM), lambda bi, h, qi: (0, h)),
        ],
        out_specs=pl.BlockSpec((1, tq, SB_HEAD_DIM), lambda bi, h, qi: (bi, qi, h)),
        out_shape=jax.ShapeDtypeStruct((b, s, SB_WIDTH), BF16),
        scratch_shapes=[pltpu.VMEM((tq, SB_HEAD_DIM), F32), pltpu.VMEM((tq, 1), F32)],
        compiler_params=_params("parallel", "parallel", "arbitrary"),
        name="sb_attention",
    )(proj, proj, proj, g.reshape(1, SB_WIDTH))


def _gla_body(q_ref, k_ref, v_ref, r_ref, la_ref, g_ref, o_ref, st_ref, *, rows):
    @pl.when(pl.program_id(2) == 0)
    def _():
        st_ref[...] = jnp.zeros_like(st_ref)

    shift = GLA_CHUNK.bit_length() - 1
    ri = lax.broadcasted_iota(I32, (rows, rows), 0)
    ci = lax.broadcasted_iota(I32, (rows, rows), 1)
    same = jnp.right_shift(ri, shift) == jnp.right_shift(ci, shift)
    tril = jnp.logical_and(same, ci <= ri)
    hi, lo = _split_bf16(la_ref[0])
    tril_b = tril.astype(BF16)
    same_b = same.astype(BF16)
    gcum = _dot(tril_b, hi) + _dot(tril_b, lo)
    glast = _dot(same_b, hi) + _dot(same_b, lo)

    q = q_ref[0].astype(F32) * (GLA_K_DIM ** -0.5)
    k = k_ref[0].astype(F32)
    v = v_ref[0]
    q_in = (q * jnp.exp(gcum)).astype(BF16)
    k_in = (k * jnp.exp(-gcum)).astype(BF16)
    k_out = (k * jnp.exp(glast - gcum)).astype(BF16)
    scores = lax.dot_general(q_in, k_in, _NT, preferred_element_type=F32)
    scores = jnp.where(tril, scores, 0.0)
    o_intra = _dot(scores.astype(BF16), v)

    state_t = st_ref[...]
    outs = []
    for n in range(rows // GLA_CHUNK):
        sl = slice(n * GLA_CHUNK, (n + 1) * GLA_CHUNK)
        o_inter = lax.dot_general(q_in[sl], state_t.astype(BF16), _NT, preferred_element_type=F32)
        outs.append(o_intra[sl] + o_inter)
        kv_t = lax.dot_general(v[sl], k_out[sl], _TN, preferred_element_type=F32)
        decay = jnp.exp(glast[n * GLA_CHUNK:n * GLA_CHUNK + 1, :])
        state_t = state_t * decay + kv_t
    st_ref[...] = state_t

    o = jnp.concatenate(outs, axis=0)
    gate = r_ref[0].astype(F32)
    y = _rms(o, g_ref[...]) * (gate * jax.nn.sigmoid(gate))
    o_ref[0] = y.astype(o_ref.dtype)


def _gla(proj, log_a, g, *, rows):
    b, s, _ = proj.shape
    return pl.pallas_call(
        functools.partial(_gla_body, rows=rows),
        grid=(b, GLA_HEADS, s // rows),
        in_specs=[
            pl.BlockSpec((1, rows, GLA_K_DIM), lambda bi, h, r: (bi, r, _GQ_BLK + h)),
            pl.BlockSpec((1, rows, GLA_K_DIM), lambda bi, h, r: (bi, r, _GK_BLK + h)),
            pl.BlockSpec((1, rows, GLA_V_DIM), lambda bi, h, r: (bi, r, _GV_BLK256 + h)),
            pl.BlockSpec((1, rows, GLA_V_DIM), lambda bi, h, r: (bi, r, _GR_BLK256 + h)),
            pl.BlockSpec((1, rows, GLA_K_DIM), lambda bi, h, r: (bi, r, h)),
            pl.BlockSpec((1, GLA_V_DIM), lambda bi, h, r: (0, h)),
        ],
        out_specs=pl.BlockSpec((1, rows, GLA_V_DIM), lambda bi, h, r: (bi, r, h)),
        out_shape=jax.ShapeDtypeStruct((b, s, GLA_V_WIDTH), BF16),
        scratch_shapes=[pltpu.VMEM((GLA_V_DIM, GLA_K_DIM), F32)],
        compiler_params=_params("parallel", "parallel", "arbitrary"),
        name="gla",
    )(proj, proj, proj, proj, log_a, g.reshape(1, GLA_V_WIDTH))


def _mix_out_body(a_ref, b_ref, wa_ref, wb_ref, x_ref, o_ref):
    o_ref[...] = x_ref[...] + _dot(a_ref[...], wa_ref[...]) + _dot(b_ref[...], wb_ref[...])


def _mix_out(sb_o, gla_o, w_out, x, *, tm, tn):
    m, n = x.shape
    ka, kb = sb_o.shape[1], gla_o.shape[1]
    assert ka == kb
    return pl.pallas_call(
        _mix_out_body,
        grid=(m // tm, n // tn),
        in_specs=[
            pl.BlockSpec((tm, ka), lambda i, j: (i, 0)),
            pl.BlockSpec((tm, kb), lambda i, j: (i, 0)),
            pl.BlockSpec((ka, tn), lambda i, j: (0, j)),
            pl.BlockSpec((kb, tn), lambda i, j: (1, j)),
            pl.BlockSpec((tm, tn), lambda i, j: (i, j)),
        ],
        out_specs=pl.BlockSpec((tm, tn), lambda i, j: (i, j)),
        out_shape=jax.ShapeDtypeStruct((m, n), F32),
        compiler_params=_params("parallel", "arbitrary"),
        name="mix_out",
    )(sb_o, gla_o, w_out, w_out, x)


def _cross_body(q_ref, kv_ref, w_ref, x_ref, o_ref, a_scr):
    @pl.when(pl.program_id(1) == 0)
    def _():
        for h in range(CROSS_HEADS):
            lo, hi = h * CROSS_HEAD_DIM, (h + 1) * CROSS_HEAD_DIM
            q = q_ref[:, lo:hi]
            k = kv_ref[0, :, lo:hi]
            v = kv_ref[0, :, D_MODEL + lo:D_MODEL + hi]
            s = lax.dot_general(q, k, _NT, preferred_element_type=F32) * (CROSS_HEAD_DIM ** -0.5)
            e = jnp.exp(s - jnp.max(s, axis=-1, keepdims=True))
            p = e / jnp.sum(e, axis=-1, keepdims=True)
            a_scr[:, lo:hi] = _dot(p.astype(BF16), v).astype(BF16)

    o_ref[...] = x_ref[...] + _dot(a_scr[...], w_ref[...])


def _cross_attention(q, kv, w_co, x, *, seq, tm, tn):
    m, n = x.shape
    mem_len = kv.shape[1]
    tiles_per_batch = seq // tm
    return pl.pallas_call(
        _cross_body,
        grid=(m // tm, n // tn),
        in_specs=[
            pl.BlockSpec((tm, n), lambda i, j: (i, 0)),
            pl.BlockSpec((1, mem_len, 2 * n), lambda i, j: (i // tiles_per_batch, 0, 0)),
            pl.BlockSpec((n, tn), lambda i, j: (0, j)),
            pl.BlockSpec((tm, tn), lambda i, j: (i, j)),
        ],
        out_specs=pl.BlockSpec((tm, tn), lambda i, j: (i, j)),
        out_shape=jax.ShapeDtypeStruct((m, n), F32),
        scratch_shapes=[pltpu.VMEM((tm, n), BF16)],
        compiler_params=_params("parallel", "arbitrary"),
        name="cross_attention",
    )(q, kv, w_co, x)


def _router_body(x_ref, g_ref, w_ref, b_ref, idx_ref, wt_ref):
    h_hi, h_lo = _split_bf16(_rms(x_ref[...], g_ref[...]))
    logits = _dot(h_hi, w_ref[0]) + _dot(h_lo, w_ref[0]) + _dot(h_hi, w_ref[1]) + b_ref[...]
    lane = lax.broadcasted_iota(I32, logits.shape, 1)
    lane_f = lane.astype(F32)
    idx_out = jnp.zeros(logits.shape, F32)
    tops = []
    for k in range(TOP_K):
        m = jnp.max(logits, axis=-1, keepdims=True)
        pick = jnp.min(jnp.where(logits == m, lane_f, float(LANES)), axis=-1, keepdims=True)
        idx_out = jnp.where(lane == k, pick, idx_out)
        tops.append(m)
        logits = jnp.where(lane_f == pick, -jnp.inf, logits)
    es = [jnp.exp(t - tops[0]) for t in tops]
    inv = 1.0 / functools.reduce(lambda a, c: a + c, es)
    wt = jnp.zeros(logits.shape, F32)
    for k in range(TOP_K):
        wt = jnp.where(lane == k, es[k] * inv, wt)
    idx_ref[...] = idx_out.astype(I32)
    wt_ref[...] = wt


def _router(x, g, w_split, b_pad, *, tm):
    m, k = x.shape
    return pl.pallas_call(
        _router_body,
        grid=(m // tm,),
        in_specs=[
            pl.BlockSpec((tm, k), lambda i: (i, 0)),
            pl.BlockSpec((1, k), lambda i: (0, 0)),
            pl.BlockSpec((2, k, LANES), lambda i: (0, 0, 0)),
            pl.BlockSpec((1, LANES), lambda i: (0, 0)),
        ],
        out_specs=[
            pl.BlockSpec((tm, LANES), lambda i: (i, 0)),
            pl.BlockSpec((tm, LANES), lambda i: (i, 0)),
        ],
        out_shape=[
            jax.ShapeDtypeStruct((m, LANES), I32),
            jax.ShapeDtypeStruct((m, LANES), F32),
        ],
        compiler_params=_params("parallel"),
        name="router",
    )(x, g.reshape(1, k), w_split, b_pad)


def _row_slice(t, pitch=SEGS):
    return pl.ds(pl.multiple_of(t * pitch, 8), SEGS)


def _col_block(s, n_tokens, first_token=0, pitch=SEGS):
    return pl.ds(first_token * pitch + s, n_tokens, stride=pitch)


def _stage_indices(idx_hbm, idx_smem, sem):
    step = pl.program_id(0)
    n = idx_smem.shape[0]
    cp = pltpu.make_async_copy(idx_hbm.at[pl.ds(pl.multiple_of(step * n, n), n)], idx_smem, sem)
    cp.start()
    cp.wait()


def _for_each_assignment(n, row_copy):
    def start(a, c):
        row_copy(a).start()
        return c

    def wait(a, c):
        row_copy(a).wait()
        return c

    lax.fori_loop(0, n, start, 0, unroll=8)
    lax.fori_loop(0, n, wait, 0, unroll=8)


def _dispatch_body(dest_hbm, x_ref, g_ref, rows_hbm, idx_smem, stage, sem_idx, sem):
    _stage_indices(dest_hbm, idx_smem, sem_idx)
    h = _rms(x_ref[...], g_ref[...])
    tm = h.shape[0]
    for s in range(SEGS):
        stage[_col_block(s, tm, pitch=ROW_PITCH), :] = h[:, s * LANES:(s + 1) * LANES]
    shift = TOP_K.bit_length() - 1

    def row_copy(a):
        return pltpu.make_async_copy(stage.at[_row_slice(jnp.right_shift(a, shift), ROW_PITCH)],
                                     rows_hbm.at[_row_slice(idx_smem[a])], sem)

    _for_each_assignment(idx_smem.shape[0], row_copy)


def _dispatch(dest, x, g, n_slots, *, tm):
    m, d = x.shape
    return pl.pallas_call(
        _dispatch_body,
        grid_spec=pltpu.PrefetchScalarGridSpec(
            num_scalar_prefetch=0,
            grid=(m // tm,),
            in_specs=[
                pl.BlockSpec(memory_space=pl.ANY),
                pl.BlockSpec((tm, d), lambda i: (i, 0)),
                pl.BlockSpec((1, d), lambda i: (0, 0)),
            ],
            out_specs=pl.BlockSpec(memory_space=pl.ANY),
            scratch_shapes=[
                pltpu.SMEM((TOP_K * tm,), I32),
                pltpu.VMEM((tm * ROW_PITCH, LANES), F32),
                pltpu.SemaphoreType.DMA,
                pltpu.SemaphoreType.DMA,
            ],
        ),
        out_shape=jax.ShapeDtypeStruct((n_slots * SEGS, LANES), F32),
        compiler_params=_params("arbitrary"),
        name="moe_dispatch",
    )(dest, x, g.reshape(1, d))


def _experts_body(blk_ref, exp_ref, rows_ref, x_ref, wg_ref, wl_ref, bg_ref, bl_ref, wd_ref, bd_ref, y_hbm,
                  xs_scr, acc_scr, y_stage, sem):
    c = pl.program_id(0)
    j = pl.program_id(1)
    last_c = pl.num_programs(0) - 1
    last_j = pl.num_programs(1) - 1
    sub_shift = MOE_SUB.bit_length() - 1

    def live_subs(chunk):
        return jnp.right_shift(rows_ref[chunk] + MOE_SUB - 1, sub_shift)

    rows = rows_ref[c]
    n_sub = live_subs(c)

    def sub_rows(r):
        return pl.ds(pl.multiple_of(r * MOE_SUB, MOE_SUB), MOE_SUB)

    def y_copy(chunk, r):
        first = (blk_ref[chunk] * MOE_CHUNK + r * MOE_SUB) * SEGS
        return pltpu.make_async_copy(y_stage.at[pl.ds(r * MOE_SUB * SEGS, MOE_SUB * SEGS)],
                                     y_hbm.at[pl.ds(pl.multiple_of(first, MOE_SUB * SEGS), MOE_SUB * SEGS)], sem)

    def for_live_subs(chunk, fn):
        n_live = live_subs(chunk)
        for r in range(MOE_CHUNK // MOE_SUB):
            @pl.when(r < n_live)
            def _():
                fn(y_copy(chunk, r))

    @pl.when(jnp.logical_and(j == 0, rows > 0))
    def _():
        def to_matrix(r, carry):
            live = r * MOE_SUB + lax.broadcasted_iota(I32, (MOE_SUB, LANES), 0) < rows
            for s in range(SEGS):
                seg = jnp.where(live, x_ref[_col_block(s, MOE_SUB, r * MOE_SUB), :], 0.0)
                xs_scr[sub_rows(r), s * LANES:(s + 1) * LANES] = seg.astype(BF16)
            return carry

        lax.fori_loop(0, n_sub, to_matrix, 0)

    wg = wg_ref[0].astype(BF16)
    wl = wl_ref[0].astype(BF16)
    wd = wd_ref[0].astype(BF16)
    bg = bg_ref[0]
    bl = bl_ref[0]

    def down(r):
        xs = xs_scr[sub_rows(r), :]
        gate = jnp.minimum(_dot(xs, wg) + bg, SWIGLU_LIMIT)
        lin = jnp.clip(_dot(xs, wl) + bl, -SWIGLU_LIMIT, SWIGLU_LIMIT)
        act = gate * jax.nn.sigmoid(SWIGLU_ALPHA * gate) * (lin + 1.0)
        return _dot(act.astype(BF16), wd)

    @pl.when(j == 0)
    def _():
        def first(r, carry):
            acc_scr[sub_rows(r), :] = bd_ref[0] + down(r)
            return carry

        lax.fori_loop(0, n_sub, first, 0)

    @pl.when(jnp.logical_and(j > 0, j < last_j))
    def _():
        def middle(r, carry):
            acc_scr[sub_rows(r), :] += down(r)
            return carry

        lax.fori_loop(0, n_sub, middle, 0)

    @pl.when(j == last_j)
    def _():
        @pl.when(jnp.logical_and(c > 0, rows > 0))
        def _():
            for_live_subs(c - 1, lambda cp: cp.wait())

        def final(r, carry):
            y = acc_scr[sub_rows(r), :] + down(r)
            for s in range(SEGS):
                y_stage[_col_block(s, MOE_SUB, r * MOE_SUB), :] = y[:, s * LANES:(s + 1) * LANES]
            return carry

        lax.fori_loop(0, n_sub, final, 0)
        for_live_subs(c, lambda cp: cp.start())

        @pl.when(c == last_c)
        def _():
            for_live_subs(blk_ref[c], lambda cp: cp.wait())


def _experts(chunk_blk, chunk_expert, chunk_rows, x_rows, w_gate_up, b_gate_up, w_down, b_down):
    n_chunks = chunk_blk.shape[0]
    d = SEGS * LANES
    n_j = D_FF // MOE_TN
    return pl.pallas_call(
        _experts_body,
        grid_spec=pltpu.PrefetchScalarGridSpec(
            num_scalar_prefetch=3,
            grid=(n_chunks, n_j),
            in_specs=[
                pl.BlockSpec((MOE_CHUNK * SEGS, LANES), lambda c, j, blk, ex, rw: (blk[c], 0),
                             pipeline_mode=pl.Buffered(1)),
                pl.BlockSpec((1, d, MOE_TN), lambda c, j, blk, ex, rw: (ex[c], 0, j)),
                pl.BlockSpec((1, d, MOE_TN), lambda c, j, blk, ex, rw: (ex[c], 0, n_j + j)),
                pl.BlockSpec((1, 1, MOE_TN), lambda c, j, blk, ex, rw: (ex[c], 0, j)),
                pl.BlockSpec((1, 1, MOE_TN), lambda c, j, blk, ex, rw: (ex[c], 0, n_j + j)),
                pl.BlockSpec((1, MOE_TN, d), lambda c, j, blk, ex, rw: (ex[c], j, 0)),
                pl.BlockSpec((1, 1, d), lambda c, j, blk, ex, rw: (ex[c], 0, 0)),
            ],
            out_specs=pl.BlockSpec(memory_space=pl.ANY),
            scratch_shapes=[
                pltpu.VMEM((MOE_CHUNK, d), BF16),
                pltpu.VMEM((MOE_CHUNK, d), F32),
                pltpu.VMEM((MOE_CHUNK * SEGS, LANES), F32),
                pltpu.SemaphoreType.DMA,
            ],
        ),
        out_shape=jax.ShapeDtypeStruct((n_chunks * MOE_CHUNK * SEGS, LANES), F32),
        compiler_params=_params("arbitrary", "arbitrary"),
        name="moe_experts",
    )(chunk_blk, chunk_expert, chunk_rows, x_rows, w_gate_up, w_gate_up,
      b_gate_up.reshape(N_EXPERTS, 1, -1), b_gate_up.reshape(N_EXPERTS, 1, -1),
      w_down, b_down.reshape(N_EXPERTS, 1, -1))


def _combine_body(dest_hbm, x_ref, wt_ref, g_ref, rows_hbm, o_ref, idx_smem, ybuf, sem_idx, sem, *, final_norm):
    _stage_indices(dest_hbm, idx_smem, sem_idx)
    shift = TOP_K.bit_length() - 1
    tm = x_ref.shape[0]

    def row_copy(a):
        slot = jnp.bitwise_and(a, TOP_K - 1) * tm + jnp.right_shift(a, shift)
        return pltpu.make_async_copy(rows_hbm.at[_row_slice(idx_smem[a])], ybuf.at[_row_slice(slot, ROW_PITCH)], sem)

    _for_each_assignment(idx_smem.shape[0], row_copy)

    wt = wt_ref[...]
    segs = []
    for s in range(SEGS):
        acc = x_ref[:, s * LANES:(s + 1) * LANES]
        for k in range(TOP_K):
            acc = acc + ybuf[_col_block(s, tm, k * tm, ROW_PITCH), :] * wt[:, k:k + 1]
        segs.append(acc)
    if final_norm:
        ssq = functools.reduce(lambda a, c: a + c, [jnp.sum(v * v, axis=-1, keepdims=True) for v in segs])
        inv = lax.rsqrt(ssq / (SEGS * LANES) + EPS)
        segs = [v * inv * g_ref[:, s * LANES:(s + 1) * LANES] for s, v in enumerate(segs)]
    for s, v in enumerate(segs):
        o_ref[:, s * LANES:(s + 1) * LANES] = v


def _combine(dest, x, wt, g, y_rows, *, tm, final_norm):
    m, d = x.shape
    return pl.pallas_call(
        functools.partial(_combine_body, final_norm=final_norm),
        grid_spec=pltpu.PrefetchScalarGridSpec(
            num_scalar_prefetch=0,
            grid=(m // tm,),
            in_specs=[
                pl.BlockSpec(memory_space=pl.ANY),
                pl.BlockSpec((tm, d), lambda i: (i, 0)),
                pl.BlockSpec((tm, LANES), lambda i: (i, 0)),
                pl.BlockSpec((1, d), lambda i: (0, 0)),
                pl.BlockSpec(memory_space=pl.ANY),
            ],
            out_specs=pl.BlockSpec((tm, d), lambda i: (i, 0)),
            scratch_shapes=[
                pltpu.SMEM((TOP_K * tm,), I32),
                pltpu.VMEM((TOP_K * tm * ROW_PITCH, LANES), F32),
                pltpu.SemaphoreType.DMA,
                pltpu.SemaphoreType.DMA,
            ],
        ),
        out_shape=jax.ShapeDtypeStruct((m, d), F32),
        compiler_params=_params("arbitrary"),
        name="moe_combine",
    )(dest, x, wt, g.reshape(1, d), y_rows)


def _routing_tables(top_idx, n_chunks):
    expert_flat = top_idx.reshape(-1)
    onehot = (expert_flat[:, None] == jnp.arange(N_EXPERTS, dtype=I32)[None, :]).astype(I32)
    running = jnp.cumsum(onehot, axis=0)
    rank = jnp.sum(onehot * running, axis=1) - 1
    counts = running[-1]
    chunks_per_expert = (counts + MOE_CHUNK - 1) // MOE_CHUNK
    chunk_end = jnp.cumsum(chunks_per_expert)
    chunk_start = chunk_end - chunks_per_expert
    dest = (chunk_start * MOE_CHUNK)[expert_flat] + rank

    c = jnp.arange(n_chunks, dtype=I32)
    used = chunk_end[-1]
    c_eff = jnp.minimum(c, used - 1)
    chunk_expert = jnp.sum((chunk_end[None, :] <= c_eff[:, None]).astype(I32), axis=1)
    rows = jnp.clip(counts[chunk_expert] - (c_eff - chunk_start[chunk_expert]) * MOE_CHUNK, 0, MOE_CHUNK)
    chunk_rows = jnp.where(c < used, rows, 0).astype(I32)
    return dest.astype(I32), c_eff.astype(I32), chunk_expert, chunk_rows


def kernel(x, mem, norm_mix_g, w_in, w_gla_gate_up, b_gla_gate, sb_norm_g, gla_norm_g, w_out,
           norm_cross_g, norm_mem_g, w_cq, w_ckv, w_co,
           norm_moe_g, w_router, b_router, w_gate_up, b_gate_up, w_down, b_down,
           norm_final_g):
    batch, seq, d = x.shape
    n_tok = batch * seq
    depth = w_in.shape[0]
    xf = x.reshape(n_tok, d)
    mem_f = mem.reshape(-1, d)
    mem_len = mem.shape[1]

    for layer in range(depth):
        w_in_l = w_in[layer]
        w_main = w_in_l[:, :PROJ_MAIN].astype(BF16)
        w_low = jnp.pad(w_in_l[:, PROJ_MAIN:], ((0, 0), (0, LANES - GLA_GATE_RANK))).astype(BF16)
        w_up = jnp.pad(w_gla_gate_up[layer], ((0, LANES - GLA_GATE_RANK), (0, 0))).astype(BF16)
        proj, log_a = _inproj(xf, norm_mix_g[layer], w_main, w_low, w_up, b_gla_gate[layer], tm=1024, tn=1024)
        proj = proj.reshape(batch, seq, PROJ_MAIN)
        sb_o = _sb_attention(proj, sb_norm_g[layer], tq=256)
        gla_o = _gla(proj, log_a.reshape(batch, seq, GLA_K_WIDTH), gla_norm_g[layer], rows=256)
        xf = _mix_out(sb_o.reshape(n_tok, SB_WIDTH), gla_o.reshape(n_tok, GLA_V_WIDTH),
                      w_out[layer].astype(BF16), xf, tm=1024, tn=1024)

        q = _norm_matmul(xf, norm_cross_g[layer], w_cq[layer].astype(BF16), tm=1024, tn=1024, out_dtype=BF16)
        kv = _norm_matmul(mem_f, norm_mem_g[layer], w_ckv[layer].astype(BF16), tm=1024, tn=1024, out_dtype=BF16)
        xf = _cross_attention(q, kv.reshape(batch, mem_len, 2 * d), w_co[layer].astype(BF16), xf,
                              seq=seq, tm=1024, tn=1024)

        w_r = jnp.pad(w_router[layer], ((0, 0), (0, LANES - N_EXPERTS)))
        w_r_hi = w_r.astype(BF16)
        w_r_lo = (w_r - w_r_hi.astype(F32)).astype(BF16)
        b_r = jnp.pad(b_router[layer], (0, LANES - N_EXPERTS), constant_values=-1e30).reshape(1, LANES)
        top_idx, top_w = _router(xf, norm_moe_g[layer], jnp.stack([w_r_hi, w_r_lo]), b_r, tm=512)

        n_chunks = n_tok * TOP_K // MOE_CHUNK + N_EXPERTS
        dest, chunk_blk, chunk_expert, chunk_rows = _routing_tables(top_idx[:, :TOP_K], n_chunks)
        x_rows = _dispatch(dest, xf, norm_moe_g[layer], n_chunks * MOE_CHUNK, tm=512)
        y_rows = _experts(chunk_blk, chunk_expert, chunk_rows, x_rows,
                          w_gate_up[layer], b_gate_up[layer], w_down[layer], b_down[layer])
        xf = _combine(dest, xf, top_w, norm_final_g, y_rows, tm=256, final_norm=layer + 1 == depth)

    return xf.reshape(batch, seq, d)
```

```python
import functools

import jax
import jax.numpy as jnp
from jax import lax
from jax.experimental import pallas as pl
from jax.experimental.pallas import tpu as pltpu

F32 = jnp.float32
BF16 = jnp.bfloat16
I32 = jnp.int32

D_MODEL = 2048
SB_HEADS = 8
SB_HEAD_DIM = 128
SB_WIDTH = SB_HEADS * SB_HEAD_DIM
GLA_HEADS = 4
GLA_K_DIM = 128
GLA_V_DIM = 256
GLA_K_WIDTH = GLA_HEADS * GLA_K_DIM
GLA_V_WIDTH = GLA_HEADS * GLA_V_DIM
GLA_GATE_RANK = 16
GLA_GATE_TAU = 16.0
GLA_CHUNK = 64
PROJ_MAIN = 3 * SB_WIDTH + 2 * GLA_K_WIDTH + 2 * GLA_V_WIDTH
CROSS_HEADS = 4
CROSS_HEAD_DIM = D_MODEL // CROSS_HEADS
N_EXPERTS = 32
TOP_K = 4
D_FF = D_MODEL
SWIGLU_LIMIT = 7.0
SWIGLU_ALPHA = 1.702
EPS = 1e-6

LANES = 128
VMEM_BYTES = 64 * 1024 * 1024
VMEM_LIMIT_BYTES = VMEM_BYTES - 8 * 1024 * 1024

SEGS = D_MODEL // LANES
ROW_PITCH = 24

_SBQ_BLK = 0
_SBK_BLK = SB_WIDTH // LANES
_SBV_BLK = 2 * SB_WIDTH // LANES
_GQ_BLK = 3 * SB_WIDTH // LANES
_GK_BLK = _GQ_BLK + GLA_K_WIDTH // LANES
_GV_BLK256 = (3 * SB_WIDTH + 2 * GLA_K_WIDTH) // GLA_V_DIM
_GR_BLK256 = _GV_BLK256 + GLA_V_WIDTH // GLA_V_DIM

SB_ZERO_LOG = -104.0

MOE_CHUNK = 1152
MOE_SUB = 384
MOE_TN = 256

_NT = (((1,), (1,)), ((), ()))
_TN = (((0,), (0,)), ((), ()))


def _params(*sem):
    return pltpu.CompilerParams(dimension_semantics=sem, vmem_limit_bytes=VMEM_LIMIT_BYTES)


def _rms(x, g):
    return x * lax.rsqrt(jnp.mean(x * x, axis=-1, keepdims=True) + EPS) * g


def _log_sigmoid(x):
    return -(jnp.maximum(-x, 0.0) + jnp.log1p(jnp.exp(-jnp.abs(x))))


def _split_bf16(x):
    hi = x.astype(BF16)
    lo = (x - hi.astype(F32)).astype(BF16)
    return hi, lo


def _dot(a, b):
    return jnp.dot(a, b, preferred_element_type=F32)


def _norm_matmul_body(x_ref, g_ref, w_ref, o_ref, h_scr):
    @pl.when(pl.program_id(1) == 0)
    def _():
        h_scr[...] = _rms(x_ref[...], g_ref[...]).astype(BF16)

    o_ref[...] = _dot(h_scr[...], w_ref[...]).astype(o_ref.dtype)


def _norm_matmul(x, g, w, *, tm, tn, out_dtype):
    m, k = x.shape
    n = w.shape[1]
    return pl.pallas_call(
        _norm_matmul_body,
        grid=(m // tm, n // tn),
        in_specs=[
            pl.BlockSpec((tm, k), lambda i, j: (i, 0)),
            pl.BlockSpec((1, k), lambda i, j: (0, 0)),
            pl.BlockSpec((k, tn), lambda i, j: (0, j)),
        ],
        out_specs=pl.BlockSpec((tm, tn), lambda i, j: (i, j)),
        out_shape=jax.ShapeDtypeStruct((m, n), out_dtype),
        scratch_shapes=[pltpu.VMEM((tm, k), BF16)],
        compiler_params=_params("parallel", "arbitrary"),
        name="norm_matmul",
    )(x, g.reshape(1, k), w)


def _inproj_body(x_ref, g_ref, w_ref, wlow_ref, wup_ref, bup_ref, o_ref, la_ref, h_scr):
    @pl.when(pl.program_id(1) == 0)
    def _():
        h = _rms(x_ref[...], g_ref[...]).astype(BF16)
        h_scr[...] = h
        low = _dot(h, wlow_ref[...])
        pre = _dot(low.astype(BF16), wup_ref[...]) + bup_ref[...]
        la_ref[...] = _log_sigmoid(pre) / GLA_GATE_TAU

    o_ref[...] = _dot(h_scr[...], w_ref[...]).astype(o_ref.dtype)


def _inproj(x, g, w_main, w_low, w_up, b_up, *, tm, tn):
    m, k = x.shape
    n = w_main.shape[1]
    return pl.pallas_call(
        _inproj_body,
        grid=(m // tm, n // tn),
        in_specs=[
            pl.BlockSpec((tm, k), lambda i, j: (i, 0)),
            pl.BlockSpec((1, k), lambda i, j: (0, 0)),
            pl.BlockSpec((k, tn), lambda i, j: (0, j)),
            pl.BlockSpec((k, LANES), lambda i, j: (0, 0)),
            pl.BlockSpec((LANES, GLA_K_WIDTH), lambda i, j: (0, 0)),
            pl.BlockSpec((1, GLA_K_WIDTH), lambda i, j: (0, 0)),
        ],
        out_specs=[
            pl.BlockSpec((tm, tn), lambda i, j: (i, j)),
            pl.BlockSpec((tm, GLA_K_WIDTH), lambda i, j: (i, 0)),
        ],
        out_shape=[
            jax.ShapeDtypeStruct((m, n), BF16),
            jax.ShapeDtypeStruct((m, GLA_K_WIDTH), F32),
        ],
        scratch_shapes=[pltpu.VMEM((tm, k), BF16)],
        compiler_params=_params("parallel", "arbitrary"),
        name="inproj",
    )(x, g.reshape(1, k), w_main, w_low, w_up, b_up.reshape(1, -1))


def _sb_body(q_ref, k_ref, v_ref, g_ref, o_ref, acc_ref, carry_ref, *, tq, heads):
    qi = pl.program_id(2)
    scale = SB_HEAD_DIM ** -0.5
    row = lax.broadcasted_iota(I32, (tq, tq), 0)
    col = lax.broadcasted_iota(I32, (tq, tq), 1)
    causal = col < row
    later = (row > col).astype(BF16)

    def tile(kj, diag):
        off = pl.multiple_of(kj * tq, tq)
        top = None
        for h in range(heads):
            cols = slice(h * SB_HEAD_DIM, (h + 1) * SB_HEAD_DIM)
            kblk = k_ref[0, pl.ds(off, tq), cols]
            vblk = v_ref[0, pl.ds(off, tq), cols]
            z = lax.dot_general(q_ref[0, :, cols], kblk, _NT, preferred_element_type=F32) * scale
            lp = jnp.log1p(jnp.exp(-jnp.abs(z)))
            log_beta = jnp.minimum(z, 0.0) - lp
            log_1m = log_beta - z
            if diag:
                log_1m = jnp.where(causal, log_1m, 0.0)
            hi, lo = _split_bf16(log_1m)
            between = _dot(hi, later) + _dot(lo, later)
            carry = carry_ref[h]
            a = jnp.exp(log_beta + between + carry)
            if diag:
                a = jnp.where(causal, a, 0.0)
            acc_ref[:, cols] += _dot(a.astype(BF16), vblk)
            carry = carry + jnp.sum(log_1m, axis=-1, keepdims=True)
            carry_ref[h] = carry
            top = jnp.max(carry) if top is None else jnp.maximum(top, jnp.max(carry))
        return top

    acc_ref[...] = jnp.zeros_like(acc_ref)
    carry_ref[...] = jnp.zeros_like(carry_ref)
    top = tile(qi, True)

    def cond(state):
        kj, top = state
        return jnp.logical_and(kj >= 0, top > SB_ZERO_LOG)

    def body(state):
        kj, _ = state
        return kj - 1, tile(kj, False)

    lax.while_loop(cond, body, (qi - 1, top))
    for h in range(heads):
        cols = slice(h * SB_HEAD_DIM, (h + 1) * SB_HEAD_DIM)
        o_ref[0, :, cols] = _rms(acc_ref[:, cols], g_ref[:, cols]).astype(o_ref.dtype)


def _sb_attention(proj, g, *, tq, heads):
    b, s, _ = proj.shape
    width = heads * SB_HEAD_DIM
    blk = lambda first: first // heads
    return pl.pallas_call(
        functools.partial(_sb_body, tq=tq, heads=heads),
        grid=(b, SB_HEADS // heads, s // tq),
        in_specs=[
            pl.BlockSpec((1, tq, width), lambda bi, h, qi: (bi, qi, blk(_SBQ_BLK) + h)),
            pl.BlockSpec((1, s, width), lambda bi, h, qi: (bi, 0, blk(_SBK_BLK) + h)),
            pl.BlockSpec((1, s, width), lambda bi, h, qi: (bi, 0, blk(_SBV_BLK) + h)),
            pl.BlockSpec((1, width), lambda bi, h, qi: (0, h)),
        ],
        out_specs=pl.BlockSpec((1, tq, width), lambda bi, h, qi: (bi, qi, h)),
        out_shape=jax.ShapeDtypeStruct((b, s, SB_WIDTH), BF16),
        scratch_shapes=[pltpu.VMEM((tq, width), F32), pltpu.VMEM((heads, tq, 1), F32)],
        compiler_params=_params("parallel", "parallel", "arbitrary"),
        name="sb_attention",
    )(proj, proj, proj, g.reshape(1, SB_WIDTH))


def _gla_body(q_ref, k_ref, v_ref, r_ref, la_ref, g_ref, o_ref, st_ref, *, rows):
    @pl.when(pl.program_id(2) == 0)
    def _():
        st_ref[...] = jnp.zeros_like(st_ref)

    shift = GLA_CHUNK.bit_length() - 1
    ri = lax.broadcasted_iota(I32, (rows, rows), 0)
    ci = lax.broadcasted_iota(I32, (rows, rows), 1)
    same = jnp.right_shift(ri, shift) == jnp.right_shift(ci, shift)
    tril = jnp.logical_and(same, ci <= ri)
    hi, lo = _split_bf16(la_ref[0])
    tril_b = tril.astype(BF16)
    same_b = same.astype(BF16)
    gcum = _dot(tril_b, hi) + _dot(tril_b, lo)
    glast = _dot(same_b, hi) + _dot(same_b, lo)

    q = q_ref[0].astype(F32) * (GLA_K_DIM ** -0.5)
    k = k_ref[0].astype(F32)
    v = v_ref[0]
    q_in = (q * jnp.exp(gcum)).astype(BF16)
    k_in = (k * jnp.exp(-gcum)).astype(BF16)
    k_out = (k * jnp.exp(glast - gcum)).astype(BF16)
    scores = lax.dot_general(q_in, k_in, _NT, preferred_element_type=F32)
    scores = jnp.where(tril, scores, 0.0)
    o_intra = _dot(scores.astype(BF16), v)

    state_t = st_ref[...]
    outs = []
    for n in range(rows // GLA_CHUNK):
        sl = slice(n * GLA_CHUNK, (n + 1) * GLA_CHUNK)
        o_inter = lax.dot_general(q_in[sl], state_t.astype(BF16), _NT, preferred_element_type=F32)
        outs.append(o_intra[sl] + o_inter)
        kv_t = lax.dot_general(v[sl], k_out[sl], _TN, preferred_element_type=F32)
        decay = jnp.exp(glast[n * GLA_CHUNK:n * GLA_CHUNK + 1, :])
        state_t = state_t * decay + kv_t
    st_ref[...] = state_t

    o = jnp.concatenate(outs, axis=0)
    gate = r_ref[0].astype(F32)
    y = _rms(o, g_ref[...]) * (gate * jax.nn.sigmoid(gate))
    o_ref[0] = y.astype(o_ref.dtype)


def _gla(proj, log_a, g, *, rows):
    b, s, _ = proj.shape
    return pl.pallas_call(
        functools.partial(_gla_body, rows=rows),
        grid=(b, GLA_HEADS, s // rows),
        in_specs=[
            pl.BlockSpec((1, rows, GLA_K_DIM), lambda bi, h, r: (bi, r, _GQ_BLK + h)),
            pl.BlockSpec((1, rows, GLA_K_DIM), lambda bi, h, r: (bi, r, _GK_BLK + h)),
            pl.BlockSpec((1, rows, GLA_V_DIM), lambda bi, h, r: (bi, r, _GV_BLK256 + h)),
            pl.BlockSpec((1, rows, GLA_V_DIM), lambda bi, h, r: (bi, r, _GR_BLK256 + h)),
            pl.BlockSpec((1, rows, GLA_K_DIM), lambda bi, h, r: (bi, r, h)),
            pl.BlockSpec((1, GLA_V_DIM), lambda bi, h, r: (0, h)),
        ],
        out_specs=pl.BlockSpec((1, rows, GLA_V_DIM), lambda bi, h, r: (bi, r, h)),
        out_shape=jax.ShapeDtypeStruct((b, s, GLA_V_WIDTH), BF16),
        scratch_shapes=[pltpu.VMEM((GLA_V_DIM, GLA_K_DIM), F32)],
        compiler_params=_params("parallel", "parallel", "arbitrary"),
        name="gla",
    )(proj, proj, proj, proj, log_a, g.reshape(1, GLA_V_WIDTH))


def _mix_out_body(a_ref, b_ref, wa_ref, wb_ref, x_ref, o_ref):
    o_ref[...] = x_ref[...] + _dot(a_ref[...], wa_ref[...]) + _dot(b_ref[...], wb_ref[...])


def _mix_out(sb_o, gla_o, w_out, x, *, tm, tn):
    m, n = x.shape
    ka, kb = sb_o.shape[1], gla_o.shape[1]
    assert ka == kb
    return pl.pallas_call(
        _mix_out_body,
        grid=(m // tm, n // tn),
        in_specs=[
            pl.BlockSpec((tm, ka), lambda i, j: (i, 0)),
            pl.BlockSpec((tm, kb), lambda i, j: (i, 0)),
            pl.BlockSpec((ka, tn), lambda i, j: (0, j)),
            pl.BlockSpec((kb, tn), lambda i, j: (1, j)),
            pl.BlockSpec((tm, tn), lambda i, j: (i, j)),
        ],
        out_specs=pl.BlockSpec((tm, tn), lambda i, j: (i, j)),
        out_shape=jax.ShapeDtypeStruct((m, n), F32),
        compiler_params=_params("parallel", "arbitrary"),
        name="mix_out",
    )(sb_o, gla_o, w_out, w_out, x)


def _cross_body(q_ref, kv_ref, w_ref, x_ref, o_ref, a_scr):
    @pl.when(pl.program_id(1) == 0)
    def _():
        for h in range(CROSS_HEADS):
            lo, hi = h * CROSS_HEAD_DIM, (h + 1) * CROSS_HEAD_DIM
            q = q_ref[:, lo:hi]
            k = kv_ref[0, :, lo:hi]
            v = kv_ref[0, :, D_MODEL + lo:D_MODEL + hi]
            s = lax.dot_general(q, k, _NT, preferred_element_type=F32) * (CROSS_HEAD_DIM ** -0.5)
            e = jnp.exp(s - jnp.max(s, axis=-1, keepdims=True))
            p = e / jnp.sum(e, axis=-1, keepdims=True)
            a_scr[:, lo:hi] = _dot(p.astype(BF16), v).astype(BF16)

    o_ref[...] = x_ref[...] + _dot(a_scr[...], w_ref[...])


def _cross_attention(q, kv, w_co, x, *, seq, tm, tn):
    m, n = x.shape
    mem_len = kv.shape[1]
    tiles_per_batch = seq // tm
    return pl.pallas_call(
        _cross_body,
        grid=(m // tm, n // tn),
        in_specs=[
            pl.BlockSpec((tm, n), lambda i, j: (i, 0)),
            pl.BlockSpec((1, mem_len, 2 * n), lambda i, j: (i // tiles_per_batch, 0, 0)),
            pl.BlockSpec((n, tn), lambda i, j: (0, j)),
            pl.BlockSpec((tm, tn), lambda i, j: (i, j)),
        ],
        out_specs=pl.BlockSpec((tm, tn), lambda i, j: (i, j)),
        out_shape=jax.ShapeDtypeStruct((m, n), F32),
        scratch_shapes=[pltpu.VMEM((tm, n), BF16)],
        compiler_params=_params("parallel", "arbitrary"),
        name="cross_attention",
    )(q, kv, w_co, x)


def _router_body(x_ref, g_ref, w_ref, b_ref, idx_ref, wt_ref):
    h_hi, h_lo = _split_bf16(_rms(x_ref[...], g_ref[...]))
    logits = _dot(h_hi, w_ref[0]) + _dot(h_lo, w_ref[0]) + _dot(h_hi, w_ref[1]) + b_ref[...]
    lane = lax.broadcasted_iota(I32, logits.shape, 1)
    lane_f = lane.astype(F32)
    idx_out = jnp.zeros(logits.shape, F32)
    tops = []
    for k in range(TOP_K):
        m = jnp.max(logits, axis=-1, keepdims=True)
        pick = jnp.min(jnp.where(logits == m, lane_f, float(LANES)), axis=-1, keepdims=True)
        idx_out = jnp.where(lane == k, pick, idx_out)
        tops.append(m)
        logits = jnp.where(lane_f == pick, -jnp.inf, logits)
    es = [jnp.exp(t - tops[0]) for t in tops]
    inv = 1.0 / functools.reduce(lambda a, c: a + c, es)
    wt = jnp.zeros(logits.shape, F32)
    for k in range(TOP_K):
        wt = jnp.where(lane == k, es[k] * inv, wt)
    idx_ref[...] = idx_out.astype(I32)
    wt_ref[...] = wt


def _router(x, g, w_split, b_pad, *, tm):
    m, k = x.shape
    return pl.pallas_call(
        _router_body,
        grid=(m // tm,),
        in_specs=[
            pl.BlockSpec((tm, k), lambda i: (i, 0)),
            pl.BlockSpec((1, k), lambda i: (0, 0)),
            pl.BlockSpec((2, k, LANES), lambda i: (0, 0, 0)),
            pl.BlockSpec((1, LANES), lambda i: (0, 0)),
        ],
        out_specs=[
            pl.BlockSpec((tm, LANES), lambda i: (i, 0)),
            pl.BlockSpec((tm, LANES), lambda i: (i, 0)),
        ],
        out_shape=[
            jax.ShapeDtypeStruct((m, LANES), I32),
            jax.ShapeDtypeStruct((m, LANES), F32),
        ],
        compiler_params=_params("parallel"),
        name="router",
    )(x, g.reshape(1, k), w_split, b_pad)


def _row_slice(t, pitch=SEGS):
    return pl.ds(pl.multiple_of(t * pitch, 8), SEGS)


def _col_block(s, n_tokens, first_token=0, pitch=SEGS):
    return pl.ds(first_token * pitch + s, n_tokens, stride=pitch)


def _stage_indices(idx_hbm, idx_smem, sem):
    step = pl.program_id(0)
    n = idx_smem.shape[0]
    cp = pltpu.make_async_copy(idx_hbm.at[pl.ds(pl.multiple_of(step * n, n), n)], idx_smem, sem)
    cp.start()
    cp.wait()


def _for_each_assignment(n, row_copy, all_rows_copy):
    def start(a, c):
        row_copy(a).start()
        return c

    lax.fori_loop(0, n, start, 0, unroll=8)
    all_rows_copy.wait()


def _dispatch_body(dest_hbm, x_ref, g_ref, rows_hbm, idx_smem, stage, sem_idx, sem):
    _stage_indices(dest_hbm, idx_smem, sem_idx)
    h = _rms(x_ref[...], g_ref[...])
    tm = h.shape[0]
    for s in range(SEGS):
        stage[_col_block(s, tm, pitch=ROW_PITCH), :] = h[:, s * LANES:(s + 1) * LANES]
    shift = TOP_K.bit_length() - 1

    def row_copy(a):
        return pltpu.make_async_copy(stage.at[_row_slice(jnp.right_shift(a, shift), ROW_PITCH)],
                                     rows_hbm.at[_row_slice(idx_smem[a])], sem)

    n = idx_smem.shape[0]
    all_rows = pltpu.make_async_copy(rows_hbm.at[pl.ds(0, n * SEGS)], rows_hbm.at[pl.ds(n * SEGS, n * SEGS)], sem)
    _for_each_assignment(n, row_copy, all_rows)


def _dispatch(dest, x, g, n_slots, *, tm):
    m, d = x.shape
    return pl.pallas_call(
        _dispatch_body,
        grid_spec=pltpu.PrefetchScalarGridSpec(
            num_scalar_prefetch=0,
            grid=(m // tm,),
            in_specs=[
                pl.BlockSpec(memory_space=pl.ANY),
                pl.BlockSpec((tm, d), lambda i: (i, 0)),
                pl.BlockSpec((1, d), lambda i: (0, 0)),
            ],
            out_specs=pl.BlockSpec(memory_space=pl.ANY),
            scratch_shapes=[
                pltpu.SMEM((TOP_K * tm,), I32),
                pltpu.VMEM((tm * ROW_PITCH, LANES), F32),
                pltpu.SemaphoreType.DMA,
                pltpu.SemaphoreType.DMA,
            ],
        ),
        out_shape=jax.ShapeDtypeStruct((n_slots * SEGS, LANES), F32),
        compiler_params=_params("arbitrary"),
        name="moe_dispatch",
    )(dest, x, g.reshape(1, d))


def _experts_body(blk_ref, exp_ref, rows_ref, x_ref, wg_ref, wl_ref, bg_ref, bl_ref, wd_ref, bd_ref, y_hbm,
                  xs_scr, acc_scr, y_stage, sem):
    c = pl.program_id(0)
    j = pl.program_id(1)
    last_c = pl.num_programs(0) - 1
    last_j = pl.num_programs(1) - 1

    def live_subs(chunk):
        return lax.div(rows_ref[chunk] + (MOE_SUB - 1), MOE_SUB)

    rows = rows_ref[c]
    n_sub = live_subs(c)

    def sub_rows(r):
        return pl.ds(pl.multiple_of(r * MOE_SUB, MOE_SUB), MOE_SUB)

    def y_copy(chunk, r):
        first = (blk_ref[chunk] * MOE_CHUNK + r * MOE_SUB) * SEGS
        return pltpu.make_async_copy(y_stage.at[pl.ds(r * MOE_SUB * SEGS, MOE_SUB * SEGS)],
                                     y_hbm.at[pl.ds(pl.multiple_of(first, MOE_SUB * SEGS), MOE_SUB * SEGS)], sem)

    def for_live_subs(chunk, fn):
        n_live = live_subs(chunk)
        for r in range(MOE_CHUNK // MOE_SUB):
            @pl.when(r < n_live)
            def _():
                fn(y_copy(chunk, r))

    @pl.when(jnp.logical_and(j == 0, rows > 0))
    def _():
        def to_matrix(r, carry):
            live = r * MOE_SUB + lax.broadcasted_iota(I32, (MOE_SUB, LANES), 0) < rows
            for s in range(SEGS):
                seg = jnp.where(live, x_ref[_col_block(s, MOE_SUB, r * MOE_SUB), :], 0.0)
                xs_scr[sub_rows(r), s * LANES:(s + 1) * LANES] = seg.astype(BF16)
            return carry

        lax.fori_loop(0, n_sub, to_matrix, 0)

    def make_down():
        wg = wg_ref[0].astype(BF16)
        wl = wl_ref[0].astype(BF16)
        wd = wd_ref[0].astype(BF16)
        bg = bg_ref[0]
        bl = bl_ref[0]

        def down(r):
            xs = xs_scr[sub_rows(r), :]
            gate = jnp.minimum(_dot(xs, wg) + bg, SWIGLU_LIMIT)
            lin = jnp.clip(_dot(xs, wl) + bl, -SWIGLU_LIMIT, SWIGLU_LIMIT)
            act = gate * jax.nn.sigmoid(SWIGLU_ALPHA * gate) * (lin + 1.0)
            return _dot(act.astype(BF16), wd)

        return down

    @pl.when(jnp.logical_and(j == 0, rows > 0))
    def _():
        down = make_down()

        def first(r, carry):
            acc_scr[sub_rows(r), :] = bd_ref[0] + down(r)
            return carry

        lax.fori_loop(0, n_sub, first, 0)

    @pl.when(jnp.logical_and(jnp.logical_and(j > 0, j < last_j), rows > 0))
    def _():
        down = make_down()

        def middle(r, carry):
            acc_scr[sub_rows(r), :] += down(r)
            return carry

        lax.fori_loop(0, n_sub, middle, 0)

    @pl.when(jnp.logical_and(j == last_j, rows > 0))
    def _():
        down = make_down()

        @pl.when(c > 0)
        def _():
            for_live_subs(c - 1, lambda cp: cp.wait())

        def final(r, carry):
            y = acc_scr[sub_rows(r), :] + down(r)
            for s in range(SEGS):
                y_stage[_col_block(s, MOE_SUB, r * MOE_SUB), :] = y[:, s * LANES:(s + 1) * LANES]
            return carry

        lax.fori_loop(0, n_sub, final, 0)
        for_live_subs(c, lambda cp: cp.start())

    @pl.when(jnp.logical_and(j == last_j, c == last_c))
    def _():
        for_live_subs(blk_ref[c], lambda cp: cp.wait())


def _experts(chunk_blk, chunk_expert, chunk_rows, x_rows, w_gate_up, b_gate_up, w_down, b_down):
    n_chunks = chunk_blk.shape[0]
    d = SEGS * LANES
    n_j = D_FF // MOE_TN
    return pl.pallas_call(
        _experts_body,
        grid_spec=pltpu.PrefetchScalarGridSpec(
            num_scalar_prefetch=3,
            grid=(n_chunks, n_j),
            in_specs=[
                pl.BlockSpec((MOE_CHUNK * SEGS, LANES), lambda c, j, blk, ex, rw: (blk[c], 0),
                             pipeline_mode=pl.Buffered(1)),
                pl.BlockSpec((1, d, MOE_TN), lambda c, j, blk, ex, rw: (ex[c], 0, j)),
                pl.BlockSpec((1, d, MOE_TN), lambda c, j, blk, ex, rw: (ex[c], 0, n_j + j)),
                pl.BlockSpec((1, 1, MOE_TN), lambda c, j, blk, ex, rw: (ex[c], 0, j)),
                pl.BlockSpec((1, 1, MOE_TN), lambda c, j, blk, ex, rw: (ex[c], 0, n_j + j)),
                pl.BlockSpec((1, MOE_TN, d), lambda c, j, blk, ex, rw: (ex[c], j, 0)),
                pl.BlockSpec((1, 1, d), lambda c, j, blk, ex, rw: (ex[c], 0, 0)),
            ],
            out_specs=pl.BlockSpec(memory_space=pl.ANY),
            scratch_shapes=[
                pltpu.VMEM((MOE_CHUNK, d), BF16),
                pltpu.VMEM((MOE_CHUNK, d), F32),
                pltpu.VMEM((MOE_CHUNK * SEGS, LANES), F32),
                pltpu.SemaphoreType.DMA,
            ],
        ),
        out_shape=jax.ShapeDtypeStruct((n_chunks * MOE_CHUNK * SEGS, LANES), F32),
        compiler_params=_params("arbitrary", "arbitrary"),
        name="moe_experts",
    )(chunk_blk, chunk_expert, chunk_rows, x_rows, w_gate_up, w_gate_up,
      b_gate_up.reshape(N_EXPERTS, 1, -1), b_gate_up.reshape(N_EXPERTS, 1, -1),
      w_down, b_down.reshape(N_EXPERTS, 1, -1))


def _combine_body(dest_hbm, x_ref, wt_ref, g_ref, rows_hbm, o_ref, idx_smem, ybuf, sem_idx, sem, *, final_norm):
    _stage_indices(dest_hbm, idx_smem, sem_idx)
    shift = TOP_K.bit_length() - 1
    tm = x_ref.shape[0]

    def row_copy(a):
        slot = jnp.bitwise_and(a, TOP_K - 1) * tm + jnp.right_shift(a, shift)
        return pltpu.make_async_copy(rows_hbm.at[_row_slice(idx_smem[a])], ybuf.at[_row_slice(slot, ROW_PITCH)], sem)

    n = idx_smem.shape[0]
    all_rows = pltpu.make_async_copy(rows_hbm.at[pl.ds(0, n * SEGS)], ybuf.at[pl.ds(0, n * SEGS)], sem)
    _for_each_assignment(n, row_copy, all_rows)

    wt = wt_ref[...]
    segs = []
    for s in range(SEGS):
        acc = x_ref[:, s * LANES:(s + 1) * LANES]
        for k in range(TOP_K):
            acc = acc + ybuf[_col_block(s, tm, k * tm, ROW_PITCH), :] * wt[:, k:k + 1]
        segs.append(acc)
    if final_norm:
        ssq = functools.reduce(lambda a, c: a + c, [jnp.sum(v * v, axis=-1, keepdims=True) for v in segs])
        inv = lax.rsqrt(ssq / (SEGS * LANES) + EPS)
        segs = [v * inv * g_ref[:, s * LANES:(s + 1) * LANES] for s, v in enumerate(segs)]
    for s, v in enumerate(segs):
        o_ref[:, s * LANES:(s + 1) * LANES] = v


def _combine(dest, x, wt, g, y_rows, *, tm, final_norm):
    m, d = x.shape
    return pl.pallas_call(
        functools.partial(_combine_body, final_norm=final_norm),
        grid_spec=pltpu.PrefetchScalarGridSpec(
            num_scalar_prefetch=0,
            grid=(m // tm,),
            in_specs=[
                pl.BlockSpec(memory_space=pl.ANY),
                pl.BlockSpec((tm, d), lambda i: (i, 0)),
                pl.BlockSpec((tm, LANES), lambda i: (i, 0)),
                pl.BlockSpec((1, d), lambda i: (0, 0)),
                pl.BlockSpec(memory_space=pl.ANY),
            ],
            out_specs=pl.BlockSpec((tm, d), lambda i: (i, 0)),
            scratch_shapes=[
                pltpu.SMEM((TOP_K * tm,), I32),
                pltpu.VMEM((TOP_K * tm * ROW_PITCH, LANES), F32),
                pltpu.SemaphoreType.DMA,
                pltpu.SemaphoreType.DMA,
            ],
        ),
        out_shape=jax.ShapeDtypeStruct((m, d), F32),
        compiler_params=_params("arbitrary"),
        name="moe_combine",
    )(dest, x, wt, g.reshape(1, d), y_rows)


def _routing_tables(top_idx, n_chunks):
    expert_flat = top_idx.reshape(-1)
    onehot = (expert_flat[:, None] == jnp.arange(N_EXPERTS, dtype=I32)[None, :]).astype(I32)
    running = jnp.cumsum(onehot, axis=0)
    rank = jnp.sum(onehot * running, axis=1) - 1
    counts = running[-1]
    chunks_per_expert = (counts + MOE_CHUNK - 1) // MOE_CHUNK
    chunk_end = jnp.cumsum(chunks_per_expert)
    chunk_start = chunk_end - chunks_per_expert
    dest = (chunk_start * MOE_CHUNK)[expert_flat] + rank

    c = jnp.arange(n_chunks, dtype=I32)
    used = chunk_end[-1]
    c_eff = jnp.minimum(c, used - 1)
    chunk_expert = jnp.sum((chunk_end[None, :] <= c_eff[:, None]).astype(I32), axis=1)
    rows = jnp.clip(counts[chunk_expert] - (c_eff - chunk_start[chunk_expert]) * MOE_CHUNK, 0, MOE_CHUNK)
    chunk_rows = jnp.where(c < used, rows, 0).astype(I32)
    return dest.astype(I32), c_eff.astype(I32), chunk_expert, chunk_rows


def kernel(x, mem, norm_mix_g, w_in, w_gla_gate_up, b_gla_gate, sb_norm_g, gla_norm_g, w_out,
           norm_cross_g, norm_mem_g, w_cq, w_ckv, w_co,
           norm_moe_g, w_router, b_router, w_gate_up, b_gate_up, w_down, b_down,
           norm_final_g):
    batch, seq, d = x.shape
    n_tok = batch * seq
    depth = w_in.shape[0]
    xf = x.reshape(n_tok, d)
    mem_f = mem.reshape(-1, d)
    mem_len = mem.shape[1]

    for layer in range(depth):
        w_in_l = w_in[layer]
        w_main = w_in_l[:, :PROJ_MAIN].astype(BF16)
        w_low = jnp.pad(w_in_l[:, PROJ_MAIN:], ((0, 0), (0, LANES - GLA_GATE_RANK))).astype(BF16)
        w_up = jnp.pad(w_gla_gate_up[layer], ((0, LANES - GLA_GATE_RANK), (0, 0))).astype(BF16)
        proj, log_a = _inproj(xf, norm_mix_g[layer], w_main, w_low, w_up, b_gla_gate[layer], tm=1024, tn=1024)
        proj = proj.reshape(batch, seq, PROJ_MAIN)
        sb_o = _sb_attention(proj, sb_norm_g[layer], tq=256, heads=4)
        gla_o = _gla(proj, log_a.reshape(batch, seq, GLA_K_WIDTH), gla_norm_g[layer], rows=256)
        xf = _mix_out(sb_o.reshape(n_tok, SB_WIDTH), gla_o.reshape(n_tok, GLA_V_WIDTH),
                      w_out[layer].astype(BF16), xf, tm=1024, tn=1024)

        q = _norm_matmul(xf, norm_cross_g[layer], w_cq[layer].astype(BF16), tm=1024, tn=1024, out_dtype=BF16)
        kv = _norm_matmul(mem_f, norm_mem_g[layer], w_ckv[layer].astype(BF16), tm=1024, tn=1024, out_dtype=BF16)
        xf = _cross_attention(q, kv.reshape(batch, mem_len, 2 * d), w_co[layer].astype(BF16), xf,
                              seq=seq, tm=1024, tn=1024)

        w_r = jnp.pad(w_router[layer], ((0, 0), (0, LANES - N_EXPERTS)))
        w_r_hi = w_r.astype(BF16)
        w_r_lo = (w_r - w_r_hi.astype(F32)).astype(BF16)
        b_r = jnp.pad(b_router[layer], (0, LANES - N_EXPERTS), constant_values=-1e30).reshape(1, LANES)
        top_idx, top_w = _router(xf, norm_moe_g[layer], jnp.stack([w_r_hi, w_r_lo]), b_r, tm=512)

        n_chunks = n_tok * TOP_K // MOE_CHUNK + N_EXPERTS
        dest, chunk_blk, chunk_expert, chunk_rows = _routing_tables(top_idx[:, :TOP_K], n_chunks)
        x_rows = _dispatch(dest, xf, norm_moe_g[layer], n_chunks * MOE_CHUNK, tm=512)
        y_rows = _experts(chunk_blk, chunk_expert, chunk_rows, x_rows,
                          w_gate_up[layer], b_gate_up[layer], w_down[layer], b_down[layer])
        xf = _combine(dest, xf, top_w, norm_final_g, y_rows, tm=256, final_norm=layer + 1 == depth)

    return xf.reshape(batch, seq, d)
```

```python
import functools

import jax
import jax.numpy as jnp
from jax import lax
from jax.experimental import pallas as pl
from jax.experimental.pallas import tpu as pltpu

F32 = jnp.float32
BF16 = jnp.bfloat16
I32 = jnp.int32

D_MODEL = 2048
SB_HEADS = 8
SB_HEAD_DIM = 128
SB_WIDTH = SB_HEADS * SB_HEAD_DIM
GLA_HEADS = 4
GLA_K_DIM = 128
GLA_V_DIM = 256
GLA_K_WIDTH = GLA_HEADS * GLA_K_DIM
GLA_V_WIDTH = GLA_HEADS * GLA_V_DIM
GLA_GATE_RANK = 16
GLA_GATE_TAU = 16.0
GLA_CHUNK = 64
PROJ_MAIN = 3 * SB_WIDTH + 2 * GLA_K_WIDTH + 2 * GLA_V_WIDTH
CROSS_HEADS = 4
CROSS_HEAD_DIM = D_MODEL // CROSS_HEADS
N_EXPERTS = 32
TOP_K = 4
D_FF = D_MODEL
SWIGLU_LIMIT = 7.0
SWIGLU_ALPHA = 1.702
EPS = 1e-6

LANES = 128
MXU_WIDTH = 256
VMEM_BYTES = 64 * 1024 * 1024
VMEM_LIMIT_BYTES = VMEM_BYTES - 8 * 1024 * 1024

SEGS = D_MODEL // LANES
ROW_PITCH = 24

_SBQ_BLK = 0
_SBK_BLK = SB_WIDTH // LANES
_SBV_BLK = 2 * SB_WIDTH // LANES
_GQ_BLK = 3 * SB_WIDTH // LANES
_GK_BLK = _GQ_BLK + GLA_K_WIDTH // LANES
_GV_BLK256 = (3 * SB_WIDTH + 2 * GLA_K_WIDTH) // GLA_V_DIM
_GR_BLK256 = _GV_BLK256 + GLA_V_WIDTH // GLA_V_DIM

SB_ZERO_LOG = -104.0

MOE_CHUNK = 1152
MOE_SUB = 384
MOE_TN = 256

_NT = (((1,), (1,)), ((), ()))
_TN = (((0,), (0,)), ((), ()))


def _params(*sem):
    return pltpu.CompilerParams(dimension_semantics=sem, vmem_limit_bytes=VMEM_LIMIT_BYTES)


def _rms(x, g):
    return x * lax.rsqrt(jnp.mean(x * x, axis=-1, keepdims=True) + EPS) * g


def _log_sigmoid(x):
    return -(jnp.maximum(-x, 0.0) + jnp.log1p(jnp.exp(-jnp.abs(x))))


def _split_bf16(x):
    hi = x.astype(BF16)
    lo = (x - hi.astype(F32)).astype(BF16)
    return hi, lo


def _dot(a, b):
    return jnp.dot(a, b, preferred_element_type=F32)


def _norm_matmul_body(x_ref, g_ref, w_ref, o_ref, h_scr):
    @pl.when(pl.program_id(1) == 0)
    def _():
        h_scr[...] = _rms(x_ref[...], g_ref[...]).astype(BF16)

    o_ref[...] = _dot(h_scr[...], w_ref[...]).astype(o_ref.dtype)


def _norm_matmul(x, g, w, *, tm, tn, out_dtype):
    m, k = x.shape
    n = w.shape[1]
    return pl.pallas_call(
        _norm_matmul_body,
        grid=(m // tm, n // tn),
        in_specs=[
            pl.BlockSpec((tm, k), lambda i, j: (i, 0)),
            pl.BlockSpec((1, k), lambda i, j: (0, 0)),
            pl.BlockSpec((k, tn), lambda i, j: (0, j)),
        ],
        out_specs=pl.BlockSpec((tm, tn), lambda i, j: (i, j)),
        out_shape=jax.ShapeDtypeStruct((m, n), out_dtype),
        scratch_shapes=[pltpu.VMEM((tm, k), BF16)],
        compiler_params=_params("parallel", "arbitrary"),
        name="norm_matmul",
    )(x, g.reshape(1, k), w)


def _inproj_body(x_ref, g_ref, w_ref, wlow_ref, wup_ref, bup_ref, o_ref, la_ref, h_scr):
    @pl.when(pl.program_id(1) == 0)
    def _():
        h = _rms(x_ref[...], g_ref[...]).astype(BF16)
        h_scr[...] = h
        low = _dot(h, wlow_ref[...])
        pre = _dot(low.astype(BF16), wup_ref[...]) + bup_ref[...]
        la_ref[...] = _log_sigmoid(pre) / GLA_GATE_TAU

    o_ref[...] = _dot(h_scr[...], w_ref[...]).astype(o_ref.dtype)


def _inproj(x, g, w_main, w_low, w_up, b_up, *, tm, tn):
    m, k = x.shape
    n = w_main.shape[1]
    return pl.pallas_call(
        _inproj_body,
        grid=(m // tm, n // tn),
        in_specs=[
            pl.BlockSpec((tm, k), lambda i, j: (i, 0)),
            pl.BlockSpec((1, k), lambda i, j: (0, 0)),
            pl.BlockSpec((k, tn), lambda i, j: (0, j)),
            pl.BlockSpec((k, LANES), lambda i, j: (0, 0)),
            pl.BlockSpec((LANES, GLA_K_WIDTH), lambda i, j: (0, 0)),
            pl.BlockSpec((1, GLA_K_WIDTH), lambda i, j: (0, 0)),
        ],
        out_specs=[
            pl.BlockSpec((tm, tn), lambda i, j: (i, j)),
            pl.BlockSpec((tm, GLA_K_WIDTH), lambda i, j: (i, 0)),
        ],
        out_shape=[
            jax.ShapeDtypeStruct((m, n), BF16),
            jax.ShapeDtypeStruct((m, GLA_K_WIDTH), F32),
        ],
        scratch_shapes=[pltpu.VMEM((tm, k), BF16)],
        compiler_params=_params("parallel", "arbitrary"),
        name="inproj",
    )(x, g.reshape(1, k), w_main, w_low, w_up, b_up.reshape(1, -1))


def _sb_body(q_ref, k_ref, v_ref, g_ref, o_ref, acc_ref, carry_ref, *, tq, heads):
    qi = pl.program_id(2)
    scale = SB_HEAD_DIM ** -0.5
    row = lax.broadcasted_iota(I32, (tq, tq), 0)
    col = lax.broadcasted_iota(I32, (tq, tq), 1)
    causal = col < row
    later = (row > col).astype(BF16)

    def tile(kj, diag):
        off = pl.multiple_of(kj * tq, tq)
        top = None
        for h in range(heads):
            cols = slice(h * SB_HEAD_DIM, (h + 1) * SB_HEAD_DIM)
            kblk = k_ref[0, pl.ds(off, tq), cols]
            vblk = v_ref[0, pl.ds(off, tq), cols]
            z = lax.dot_general(q_ref[0, :, cols], kblk, _NT, preferred_element_type=F32) * scale
            lp = jnp.log1p(jnp.exp(-jnp.abs(z)))
            log_beta = jnp.minimum(z, 0.0) - lp
            log_1m = log_beta - z
            if diag:
                log_1m = jnp.where(causal, log_1m, 0.0)
            hi, lo = _split_bf16(log_1m)
            between = _dot(hi, later) + _dot(lo, later)
            carry = carry_ref[h]
            a = jnp.exp(log_beta + between + carry)
            if diag:
                a = jnp.where(causal, a, 0.0)
            acc_ref[:, cols] += _dot(a.astype(BF16), vblk)
            carry = carry + jnp.sum(log_1m, axis=-1, keepdims=True)
            carry_ref[h] = carry
            top = jnp.max(carry) if top is None else jnp.maximum(top, jnp.max(carry))
        return top

    acc_ref[...] = jnp.zeros_like(acc_ref)
    carry_ref[...] = jnp.zeros_like(carry_ref)
    top = tile(qi, True)

    def cond(state):
        kj, top = state
        return jnp.logical_and(kj >= 0, top > SB_ZERO_LOG)

    def body(state):
        kj, _ = state
        return kj - 1, tile(kj, False)

    lax.while_loop(cond, body, (qi - 1, top))
    for h in range(heads):
        cols = slice(h * SB_HEAD_DIM, (h + 1) * SB_HEAD_DIM)
        o_ref[0, :, cols] = _rms(acc_ref[:, cols], g_ref[:, cols]).astype(o_ref.dtype)


def _sb_attention(proj, g, *, tq, heads):
    b, s, _ = proj.shape
    width = heads * SB_HEAD_DIM
    blk = lambda first: first // heads
    return pl.pallas_call(
        functools.partial(_sb_body, tq=tq, heads=heads),
        grid=(b, SB_HEADS // heads, s // tq),
        in_specs=[
            pl.BlockSpec((1, tq, width), lambda bi, h, qi: (bi, qi, blk(_SBQ_BLK) + h)),
            pl.BlockSpec((1, s, width), lambda bi, h, qi: (bi, 0, blk(_SBK_BLK) + h)),
            pl.BlockSpec((1, s, width), lambda bi, h, qi: (bi, 0, blk(_SBV_BLK) + h)),
            pl.BlockSpec((1, width), lambda bi, h, qi: (0, h)),
        ],
        out_specs=pl.BlockSpec((1, tq, width), lambda bi, h, qi: (bi, qi, h)),
        out_shape=jax.ShapeDtypeStruct((b, s, SB_WIDTH), BF16),
        scratch_shapes=[pltpu.VMEM((tq, width), F32), pltpu.VMEM((heads, tq, 1), F32)],
        compiler_params=_params("parallel", "parallel", "arbitrary"),
        name="sb_attention",
    )(proj, proj, proj, g.reshape(1, SB_WIDTH))


def _gla_body(q_ref, k_ref, v_ref, r_ref, la_ref, g_ref, o_ref, st_ref, *, rows):
    @pl.when(pl.program_id(2) == 0)
    def _():
        st_ref[...] = jnp.zeros_like(st_ref)

    shift = GLA_CHUNK.bit_length() - 1
    ri = lax.broadcasted_iota(I32, (rows, rows), 0)
    ci = lax.broadcasted_iota(I32, (rows, rows), 1)
    same = jnp.right_shift(ri, shift) == jnp.right_shift(ci, shift)
    tril = jnp.logical_and(same, ci <= ri)
    hi, lo = _split_bf16(la_ref[0])
    tril_b = tril.astype(BF16)
    same_b = same.astype(BF16)
    gcum = _dot(tril_b, hi) + _dot(tril_b, lo)
    glast = _dot(same_b, hi) + _dot(same_b, lo)

    q = q_ref[0].astype(F32) * (GLA_K_DIM ** -0.5)
    k = k_ref[0].astype(F32)
    v = v_ref[0]
    q_in = (q * jnp.exp(gcum)).astype(BF16)
    k_in = (k * jnp.exp(-gcum)).astype(BF16)
    k_out = (k * jnp.exp(glast - gcum)).astype(BF16)
    scores = lax.dot_general(q_in, k_in, _NT, preferred_element_type=F32)
    scores = jnp.where(tril, scores, 0.0)
    o_intra = _dot(scores.astype(BF16), v)

    state_t = st_ref[...]
    outs = []
    for n in range(rows // GLA_CHUNK):
        sl = slice(n * GLA_CHUNK, (n + 1) * GLA_CHUNK)
        o_inter = lax.dot_general(q_in[sl], state_t.astype(BF16), _NT, preferred_element_type=F32)
        outs.append(o_intra[sl] + o_inter)
        kv_t = lax.dot_general(v[sl], k_out[sl], _TN, preferred_element_type=F32)
        decay = jnp.exp(glast[n * GLA_CHUNK:n * GLA_CHUNK + 1, :])
        state_t = state_t * decay + kv_t
    st_ref[...] = state_t

    o = jnp.concatenate(outs, axis=0)
    gate = r_ref[0].astype(F32)
    y = _rms(o, g_ref[...]) * (gate * jax.nn.sigmoid(gate))
    o_ref[0] = y.astype(o_ref.dtype)


def _gla(proj, log_a, g, *, rows):
    b, s, _ = proj.shape
    return pl.pallas_call(
        functools.partial(_gla_body, rows=rows),
        grid=(b, GLA_HEADS, s // rows),
        in_specs=[
            pl.BlockSpec((1, rows, GLA_K_DIM), lambda bi, h, r: (bi, r, _GQ_BLK + h)),
            pl.BlockSpec((1, rows, GLA_K_DIM), lambda bi, h, r: (bi, r, _GK_BLK + h)),
            pl.BlockSpec((1, rows, GLA_V_DIM), lambda bi, h, r: (bi, r, _GV_BLK256 + h)),
            pl.BlockSpec((1, rows, GLA_V_DIM), lambda bi, h, r: (bi, r, _GR_BLK256 + h)),
            pl.BlockSpec((1, rows, GLA_K_DIM), lambda bi, h, r: (bi, r, h)),
            pl.BlockSpec((1, GLA_V_DIM), lambda bi, h, r: (0, h)),
        ],
        out_specs=pl.BlockSpec((1, rows, GLA_V_DIM), lambda bi, h, r: (bi, r, h)),
        out_shape=jax.ShapeDtypeStruct((b, s, GLA_V_WIDTH), BF16),
        scratch_shapes=[pltpu.VMEM((GLA_V_DIM, GLA_K_DIM), F32)],
        compiler_params=_params("parallel", "parallel", "arbitrary"),
        name="gla",
    )(proj, proj, proj, proj, log_a, g.reshape(1, GLA_V_WIDTH))


def _mix_out_body(a_ref, b_ref, wa_ref, wb_ref, x_ref, o_ref):
    o_ref[...] = x_ref[...] + _dot(a_ref[...], wa_ref[...]) + _dot(b_ref[...], wb_ref[...])


def _mix_out(sb_o, gla_o, w_out, x, *, tm, tn):
    m, n = x.shape
    ka, kb = sb_o.shape[1], gla_o.shape[1]
    assert ka == kb
    return pl.pallas_call(
        _mix_out_body,
        grid=(m // tm, n // tn),
        in_specs=[
            pl.BlockSpec((tm, ka), lambda i, j: (i, 0)),
            pl.BlockSpec((tm, kb), lambda i, j: (i, 0)),
            pl.BlockSpec((ka, tn), lambda i, j: (0, j)),
            pl.BlockSpec((kb, tn), lambda i, j: (1, j)),
            pl.BlockSpec((tm, tn), lambda i, j: (i, j)),
        ],
        out_specs=pl.BlockSpec((tm, tn), lambda i, j: (i, j)),
        out_shape=jax.ShapeDtypeStruct((m, n), F32),
        compiler_params=_params("parallel", "arbitrary"),
        name="mix_out",
    )(sb_o, gla_o, w_out, w_out, x)


def _cross_body(q_ref, kv_ref, w_ref, x_ref, o_ref, a_scr):
    @pl.when(pl.program_id(1) == 0)
    def _():
        for h in range(CROSS_HEADS):
            lo, hi = h * CROSS_HEAD_DIM, (h + 1) * CROSS_HEAD_DIM
            q = q_ref[:, lo:hi]
            k = kv_ref[0, :, lo:hi]
            v = kv_ref[0, :, D_MODEL + lo:D_MODEL + hi]
            s = lax.dot_general(q, k, _NT, preferred_element_type=F32) * (CROSS_HEAD_DIM ** -0.5)
            e = jnp.exp(s - jnp.max(s, axis=-1, keepdims=True))
            p = e / jnp.sum(e, axis=-1, keepdims=True)
            a_scr[:, lo:hi] = _dot(p.astype(BF16), v).astype(BF16)

    o_ref[...] = x_ref[...] + _dot(a_scr[...], w_ref[...])


def _cross_attention(q, kv, w_co, x, *, seq, tm, tn):
    m, n = x.shape
    mem_len = kv.shape[1]
    tiles_per_batch = seq // tm
    return pl.pallas_call(
        _cross_body,
        grid=(m // tm, n // tn),
        in_specs=[
            pl.BlockSpec((tm, n), lambda i, j: (i, 0)),
            pl.BlockSpec((1, mem_len, 2 * n), lambda i, j: (i // tiles_per_batch, 0, 0)),
            pl.BlockSpec((n, tn), lambda i, j: (0, j)),
            pl.BlockSpec((tm, tn), lambda i, j: (i, j)),
        ],
        out_specs=pl.BlockSpec((tm, tn), lambda i, j: (i, j)),
        out_shape=jax.ShapeDtypeStruct((m, n), F32),
        scratch_shapes=[pltpu.VMEM((tm, n), BF16)],
        compiler_params=_params("parallel", "arbitrary"),
        name="cross_attention",
    )(q, kv, w_co, x)


def _router_body(x_ref, g_ref, w_ref, b_ref, idx_ref, wt_ref):
    h_hi, h_lo = _split_bf16(_rms(x_ref[...], g_ref[...]))
    logits = _dot(h_hi, w_ref[0]) + _dot(h_lo, w_ref[0]) + _dot(h_hi, w_ref[1]) + b_ref[...]
    lane = lax.broadcasted_iota(I32, logits.shape, 1)
    lane_f = lane.astype(F32)
    idx_out = jnp.zeros(logits.shape, F32)
    tops = []
    for k in range(TOP_K):
        m = jnp.max(logits, axis=-1, keepdims=True)
        pick = jnp.min(jnp.where(logits == m, lane_f, float(LANES)), axis=-1, keepdims=True)
        idx_out = jnp.where(lane == k, pick, idx_out)
        tops.append(m)
        logits = jnp.where(lane_f == pick, -jnp.inf, logits)
    es = [jnp.exp(t - tops[0]) for t in tops]
    inv = 1.0 / functools.reduce(lambda a, c: a + c, es)
    wt = jnp.zeros(logits.shape, F32)
    for k in range(TOP_K):
        wt = jnp.where(lane == k, es[k] * inv, wt)
    idx_ref[...] = idx_out.astype(I32)
    wt_ref[...] = wt


def _router(x, g, w_split, b_pad, *, tm):
    m, k = x.shape
    return pl.pallas_call(
        _router_body,
        grid=(m // tm,),
        in_specs=[
            pl.BlockSpec((tm, k), lambda i: (i, 0)),
            pl.BlockSpec((1, k), lambda i: (0, 0)),
            pl.BlockSpec((2, k, LANES), lambda i: (0, 0, 0)),
            pl.BlockSpec((1, LANES), lambda i: (0, 0)),
        ],
        out_specs=[
            pl.BlockSpec((tm, LANES), lambda i: (i, 0)),
            pl.BlockSpec((tm, LANES), lambda i: (i, 0)),
        ],
        out_shape=[
            jax.ShapeDtypeStruct((m, LANES), I32),
            jax.ShapeDtypeStruct((m, LANES), F32),
        ],
        compiler_params=_params("parallel"),
        name="router",
    )(x, g.reshape(1, k), w_split, b_pad)


def _row_slice(t, pitch=SEGS):
    return pl.ds(pl.multiple_of(t * pitch, 8), SEGS)


def _col_block(s, n_tokens, first_token=0, pitch=SEGS):
    return pl.ds(first_token * pitch + s, n_tokens, stride=pitch)


def _stage_indices(idx_hbm, idx_smem, sem):
    step = pl.program_id(0)
    n = idx_smem.shape[0]
    cp = pltpu.make_async_copy(idx_hbm.at[pl.ds(pl.multiple_of(step * n, n), n)], idx_smem, sem)
    cp.start()
    cp.wait()


def _for_each_assignment(n, row_copy, all_rows_copy):
    def start(a, c):
        row_copy(a).start()
        return c

    lax.fori_loop(0, n, start, 0, unroll=8)
    all_rows_copy.wait()


def _dispatch_body(dest_hbm, x_ref, g_ref, rows_hbm, idx_smem, stage, sem_idx, sem):
    _stage_indices(dest_hbm, idx_smem, sem_idx)
    h = _rms(x_ref[...], g_ref[...])
    tm = h.shape[0]
    for s in range(SEGS):
        stage[_col_block(s, tm, pitch=ROW_PITCH), :] = h[:, s * LANES:(s + 1) * LANES]
    shift = TOP_K.bit_length() - 1

    def row_copy(a):
        return pltpu.make_async_copy(stage.at[_row_slice(jnp.right_shift(a, shift), ROW_PITCH)],
                                     rows_hbm.at[_row_slice(idx_smem[a])], sem)

    n = idx_smem.shape[0]
    all_rows = pltpu.make_async_copy(rows_hbm.at[pl.ds(0, n * SEGS)], rows_hbm.at[pl.ds(n * SEGS, n * SEGS)], sem)
    _for_each_assignment(n, row_copy, all_rows)


def _dispatch(dest, x, g, n_slots, *, tm):
    m, d = x.shape
    return pl.pallas_call(
        _dispatch_body,
        grid_spec=pltpu.PrefetchScalarGridSpec(
            num_scalar_prefetch=0,
            grid=(m // tm,),
            in_specs=[
                pl.BlockSpec(memory_space=pl.ANY),
                pl.BlockSpec((tm, d), lambda i: (i, 0)),
                pl.BlockSpec((1, d), lambda i: (0, 0)),
            ],
            out_specs=pl.BlockSpec(memory_space=pl.ANY),
            scratch_shapes=[
                pltpu.SMEM((TOP_K * tm,), I32),
                pltpu.VMEM((tm * ROW_PITCH, LANES), F32),
                pltpu.SemaphoreType.DMA,
                pltpu.SemaphoreType.DMA,
            ],
        ),
        out_shape=jax.ShapeDtypeStruct((n_slots * SEGS, LANES), F32),
        compiler_params=_params("arbitrary"),
        name="moe_dispatch",
    )(dest, x, g.reshape(1, d))


def _experts_body(blk_ref, exp_ref, rows_ref, x_ref, wg_ref, wl_ref, bg_ref, bl_ref, wd_ref, bd_ref, y_hbm,
                  xs_scr, acc_scr, y_stage, sem):
    c = pl.program_id(0)
    j = pl.program_id(1)
    last_c = pl.num_programs(0) - 1
    last_j = pl.num_programs(1) - 1

    def live_subs(chunk):
        return lax.div(rows_ref[chunk] + (MOE_SUB - 1), MOE_SUB)

    rows = rows_ref[c]

    def y_copy(chunk, r):
        first = (blk_ref[chunk] * MOE_CHUNK + r * MOE_SUB) * SEGS
        return pltpu.make_async_copy(y_stage.at[pl.ds(r * MOE_SUB * SEGS, MOE_SUB * SEGS)],
                                     y_hbm.at[pl.ds(pl.multiple_of(first, MOE_SUB * SEGS), MOE_SUB * SEGS)], sem)

    def for_live_subs(chunk, fn):
        n_live = live_subs(chunk)
        for r in range(MOE_CHUNK // MOE_SUB):
            @pl.when(r < n_live)
            def _():
                fn(y_copy(chunk, r))

    @pl.when(jnp.logical_and(j == 0, rows > 0))
    def _():
        for r in range(MOE_CHUNK // MOE_SUB):
            live = r * MOE_SUB + lax.broadcasted_iota(I32, (MOE_SUB, LANES), 0) < rows
            for s in range(SEGS):
                seg = jnp.where(live, x_ref[_col_block(s, MOE_SUB, r * MOE_SUB), :], 0.0)
                xs_scr[r * MOE_SUB:(r + 1) * MOE_SUB, s * LANES:(s + 1) * LANES] = seg.astype(BF16)

    def hidden():
        xs = xs_scr[...]
        gate = jnp.minimum(_dot(xs, wg_ref[0].astype(BF16)) + bg_ref[0], SWIGLU_LIMIT)
        lin = jnp.clip(_dot(xs, wl_ref[0].astype(BF16)) + bl_ref[0], -SWIGLU_LIMIT, SWIGLU_LIMIT)
        return (gate * jax.nn.sigmoid(SWIGLU_ALPHA * gate) * (lin + 1.0)).astype(BF16)

    def down_tiles(act):
        for n in range(acc_scr.shape[1] // MXU_WIDTH):
            cols = slice(n * MXU_WIDTH, (n + 1) * MXU_WIDTH)
            yield cols, _dot(act, wd_ref[0, :, cols].astype(BF16))

    @pl.when(jnp.logical_and(j == 0, rows > 0))
    def _():
        for cols, out in down_tiles(hidden()):
            acc_scr[:, cols] = bd_ref[0, :, cols] + out

    @pl.when(jnp.logical_and(jnp.logical_and(j > 0, j < last_j), rows > 0))
    def _():
        for cols, out in down_tiles(hidden()):
            acc_scr[:, cols] += out

    @pl.when(jnp.logical_and(j == last_j, rows > 0))
    def _():
        @pl.when(c > 0)
        def _():
            for_live_subs(c - 1, lambda cp: cp.wait())

        for cols, out in down_tiles(hidden()):
            y = acc_scr[:, cols] + out
            for i in range(MXU_WIDTH // LANES):
                s = cols.start // LANES + i
                y_stage[_col_block(s, MOE_CHUNK), :] = y[:, i * LANES:(i + 1) * LANES]
        for_live_subs(c, lambda cp: cp.start())


    @pl.when(jnp.logical_and(j == last_j, c == last_c))
    def _():
        for_live_subs(blk_ref[c], lambda cp: cp.wait())


def _experts(chunk_blk, chunk_expert, chunk_rows, x_rows, w_gate_up, b_gate_up, w_down, b_down):
    n_chunks = chunk_blk.shape[0]
    d = SEGS * LANES
    n_j = D_FF // MOE_TN
    return pl.pallas_call(
        _experts_body,
        grid_spec=pltpu.PrefetchScalarGridSpec(
            num_scalar_prefetch=3,
            grid=(n_chunks, n_j),
            in_specs=[
                pl.BlockSpec((MOE_CHUNK * SEGS, LANES), lambda c, j, blk, ex, rw: (blk[c], 0),
                             pipeline_mode=pl.Buffered(1)),
                pl.BlockSpec((1, d, MOE_TN), lambda c, j, blk, ex, rw: (ex[c], 0, j)),
                pl.BlockSpec((1, d, MOE_TN), lambda c, j, blk, ex, rw: (ex[c], 0, n_j + j)),
                pl.BlockSpec((1, 1, MOE_TN), lambda c, j, blk, ex, rw: (ex[c], 0, j)),
                pl.BlockSpec((1, 1, MOE_TN), lambda c, j, blk, ex, rw: (ex[c], 0, n_j + j)),
                pl.BlockSpec((1, MOE_TN, d), lambda c, j, blk, ex, rw: (ex[c], j, 0)),
                pl.BlockSpec((1, 1, d), lambda c, j, blk, ex, rw: (ex[c], 0, 0)),
            ],
            out_specs=pl.BlockSpec(memory_space=pl.ANY),
            scratch_shapes=[
                pltpu.VMEM((MOE_CHUNK, d), BF16),
                pltpu.VMEM((MOE_CHUNK, d), F32),
                pltpu.VMEM((MOE_CHUNK * SEGS, LANES), F32),
                pltpu.SemaphoreType.DMA,
            ],
        ),
        out_shape=jax.ShapeDtypeStruct((n_chunks * MOE_CHUNK * SEGS, LANES), F32),
        compiler_params=_params("arbitrary", "arbitrary"),
        name="moe_experts",
    )(chunk_blk, chunk_expert, chunk_rows, x_rows, w_gate_up, w_gate_up,
      b_gate_up.reshape(N_EXPERTS, 1, -1), b_gate_up.reshape(N_EXPERTS, 1, -1),
      w_down, b_down.reshape(N_EXPERTS, 1, -1))


def _combine_body(dest_hbm, x_ref, wt_ref, g_ref, rows_hbm, o_ref, idx_smem, ybuf, sem_idx, sem, *, final_norm):
    _stage_indices(dest_hbm, idx_smem, sem_idx)
    shift = TOP_K.bit_length() - 1
    tm = x_ref.shape[0]

    def row_copy(a):
        slot = jnp.bitwise_and(a, TOP_K - 1) * tm + jnp.right_shift(a, shift)
        return pltpu.make_async_copy(rows_hbm.at[_row_slice(idx_smem[a])], ybuf.at[_row_slice(slot, ROW_PITCH)], sem)

    n = idx_smem.shape[0]
    all_rows = pltpu.make_async_copy(rows_hbm.at[pl.ds(0, n * SEGS)], ybuf.at[pl.ds(0, n * SEGS)], sem)
    _for_each_assignment(n, row_copy, all_rows)

    wt = wt_ref[...]
    segs = []
    for s in range(SEGS):
        acc = x_ref[:, s * LANES:(s + 1) * LANES]
        for k in range(TOP_K):
            acc = acc + ybuf[_col_block(s, tm, k * tm, ROW_PITCH), :] * wt[:, k:k + 1]
        segs.append(acc)
    if final_norm:
        ssq = functools.reduce(lambda a, c: a + c, [jnp.sum(v * v, axis=-1, keepdims=True) for v in segs])
        inv = lax.rsqrt(ssq / (SEGS * LANES) + EPS)
        segs = [v * inv * g_ref[:, s * LANES:(s + 1) * LANES] for s, v in enumerate(segs)]
    for s, v in enumerate(segs):
        o_ref[:, s * LANES:(s + 1) * LANES] = v


def _combine(dest, x, wt, g, y_rows, *, tm, final_norm):
    m, d = x.shape
    return pl.pallas_call(
        functools.partial(_combine_body, final_norm=final_norm),
        grid_spec=pltpu.PrefetchScalarGridSpec(
            num_scalar_prefetch=0,
            grid=(m // tm,),
            in_specs=[
                pl.BlockSpec(memory_space=pl.ANY),
                pl.BlockSpec((tm, d), lambda i: (i, 0)),
                pl.BlockSpec((tm, LANES), lambda i: (i, 0)),
                pl.BlockSpec((1, d), lambda i: (0, 0)),
                pl.BlockSpec(memory_space=pl.ANY),
            ],
            out_specs=pl.BlockSpec((tm, d), lambda i: (i, 0)),
            scratch_shapes=[
                pltpu.SMEM((TOP_K * tm,), I32),
                pltpu.VMEM((TOP_K * tm * ROW_PITCH, LANES), F32),
                pltpu.SemaphoreType.DMA,
                pltpu.SemaphoreType.DMA,
            ],
        ),
        out_shape=jax.ShapeDtypeStruct((m, d), F32),
        compiler_params=_params("arbitrary"),
        name="moe_combine",
    )(dest, x, wt, g.reshape(1, d), y_rows)


def _routing_tables(top_idx, n_chunks):
    expert_flat = top_idx.reshape(-1)
    onehot = (expert_flat[:, None] == jnp.arange(N_EXPERTS, dtype=I32)[None, :]).astype(I32)
    running = jnp.cumsum(onehot, axis=0)
    rank = jnp.sum(onehot * running, axis=1) - 1
    counts = running[-1]
    chunks_per_expert = (counts + MOE_CHUNK - 1) // MOE_CHUNK
    chunk_end = jnp.cumsum(chunks_per_expert)
    chunk_start = chunk_end - chunks_per_expert
    dest = (chunk_start * MOE_CHUNK)[expert_flat] + rank

    c = jnp.arange(n_chunks, dtype=I32)
    used = chunk_end[-1]
    c_eff = jnp.minimum(c, used - 1)
    chunk_expert = jnp.sum((chunk_end[None, :] <= c_eff[:, None]).astype(I32), axis=1)
    rows = jnp.clip(counts[chunk_expert] - (c_eff - chunk_start[chunk_expert]) * MOE_CHUNK, 0, MOE_CHUNK)
    chunk_rows = jnp.where(c < used, rows, 0).astype(I32)
    return dest.astype(I32), c_eff.astype(I32), chunk_expert, chunk_rows


def kernel(x, mem, norm_mix_g, w_in, w_gla_gate_up, b_gla_gate, sb_norm_g, gla_norm_g, w_out,
           norm_cross_g, norm_mem_g, w_cq, w_ckv, w_co,
           norm_moe_g, w_router, b_router, w_gate_up, b_gate_up, w_down, b_down,
           norm_final_g):
    batch, seq, d = x.shape
    n_tok = batch * seq
    depth = w_in.shape[0]
    xf = x.reshape(n_tok, d)
    mem_f = mem.reshape(-1, d)
    mem_len = mem.shape[1]

    for layer in range(depth):
        w_in_l = w_in[layer]
        w_main = w_in_l[:, :PROJ_MAIN].astype(BF16)
        w_low = jnp.pad(w_in_l[:, PROJ_MAIN:], ((0, 0), (0, LANES - GLA_GATE_RANK))).astype(BF16)
        w_up = jnp.pad(w_gla_gate_up[layer], ((0, LANES - GLA_GATE_RANK), (0, 0))).astype(BF16)
        proj, log_a = _inproj(xf, norm_mix_g[layer], w_main, w_low, w_up, b_gla_gate[layer], tm=1024, tn=1024)
        proj = proj.reshape(batch, seq, PROJ_MAIN)
        sb_o = _sb_attention(proj, sb_norm_g[layer], tq=256, heads=4)
        gla_o = _gla(proj, log_a.reshape(batch, seq, GLA_K_WIDTH), gla_norm_g[layer], rows=256)
        xf = _mix_out(sb_o.reshape(n_tok, SB_WIDTH), gla_o.reshape(n_tok, GLA_V_WIDTH),
                      w_out[layer].astype(BF16), xf, tm=1024, tn=1024)

        q = _norm_matmul(xf, norm_cross_g[layer], w_cq[layer].astype(BF16), tm=1024, tn=1024, out_dtype=BF16)
        kv = _norm_matmul(mem_f, norm_mem_g[layer], w_ckv[layer].astype(BF16), tm=1024, tn=1024, out_dtype=BF16)
        xf = _cross_attention(q, kv.reshape(batch, mem_len, 2 * d), w_co[layer].astype(BF16), xf,
                              seq=seq, tm=1024, tn=1024)

        w_r = jnp.pad(w_router[layer], ((0, 0), (0, LANES - N_EXPERTS)))
        w_r_hi = w_r.astype(BF16)
        w_r_lo = (w_r - w_r_hi.astype(F32)).astype(BF16)
        b_r = jnp.pad(b_router[layer], (0, LANES - N_EXPERTS), constant_values=-1e30).reshape(1, LANES)
        top_idx, top_w = _router(xf, norm_moe_g[layer], jnp.stack([w_r_hi, w_r_lo]), b_r, tm=512)

        n_chunks = n_tok * TOP_K // MOE_CHUNK + N_EXPERTS
        dest, chunk_blk, chunk_expert, chunk_rows = _routing_tables(top_idx[:, :TOP_K], n_chunks)
        x_rows = _dispatch(dest, xf, norm_moe_g[layer], n_chunks * MOE_CHUNK, tm=512)
        y_rows = _experts(chunk_blk, chunk_expert, chunk_rows, x_rows,
                          w_gate_up[layer], b_gate_up[layer], w_down[layer], b_down[layer])
        xf = _combine(dest, xf, top_w, norm_final_g, y_rows, tm=256, final_norm=layer + 1 == depth)

    return xf.reshape(batch, seq, d)
```

```python
import functools

import jax
import jax.numpy as jnp
from jax import lax
from jax.experimental import pallas as pl
from jax.experimental.pallas import tpu as pltpu

F32 = jnp.float32
BF16 = jnp.bfloat16
I32 = jnp.int32

D_MODEL = 2048
SB_HEADS = 8
SB_HEAD_DIM = 128
SB_WIDTH = SB_HEADS * SB_HEAD_DIM
GLA_HEADS = 4
GLA_K_DIM = 128
GLA_V_DIM = 256
GLA_K_WIDTH = GLA_HEADS * GLA_K_DIM
GLA_V_WIDTH = GLA_HEADS * GLA_V_DIM
GLA_GATE_RANK = 16
GLA_GATE_TAU = 16.0
GLA_CHUNK = 64
PROJ_MAIN = 3 * SB_WIDTH + 2 * GLA_K_WIDTH + 2 * GLA_V_WIDTH
CROSS_HEADS = 4
CROSS_HEAD_DIM = D_MODEL // CROSS_HEADS
N_EXPERTS = 32
TOP_K = 4
D_FF = D_MODEL
SWIGLU_LIMIT = 7.0
SWIGLU_ALPHA = 1.702
EPS = 1e-6

LANES = 128
MXU_WIDTH = 256
VMEM_BYTES = 64 * 1024 * 1024
VMEM_LIMIT_BYTES = VMEM_BYTES - 8 * 1024 * 1024

SEGS = D_MODEL // LANES
ROW_PITCH = 24

_SBQ_BLK = 0
_SBK_BLK = SB_WIDTH // LANES
_SBV_BLK = 2 * SB_WIDTH // LANES
_GQ_BLK = 3 * SB_WIDTH // LANES
_GK_BLK = _GQ_BLK + GLA_K_WIDTH // LANES
_GV_BLK256 = (3 * SB_WIDTH + 2 * GLA_K_WIDTH) // GLA_V_DIM
_GR_BLK256 = _GV_BLK256 + GLA_V_WIDTH // GLA_V_DIM

SB_ZERO_LOG = -104.0

MOE_CHUNK = 1152
MOE_SUB = 384
MOE_TN = 256

_NT = (((1,), (1,)), ((), ()))
_TN = (((0,), (0,)), ((), ()))


def _params(*sem):
    return pltpu.CompilerParams(dimension_semantics=sem, vmem_limit_bytes=VMEM_LIMIT_BYTES)


def _rms(x, g):
    return x * lax.rsqrt(jnp.mean(x * x, axis=-1, keepdims=True) + EPS) * g


def _log_sigmoid(x):
    return -(jnp.maximum(-x, 0.0) + jnp.log1p(jnp.exp(-jnp.abs(x))))


def _split_bf16(x):
    hi = x.astype(BF16)
    lo = (x - hi.astype(F32)).astype(BF16)
    return hi, lo


def _dot(a, b):
    return jnp.dot(a, b, preferred_element_type=F32)


def _norm_matmul_body(x_ref, g_ref, w_ref, o_ref, h_scr):
    @pl.when(pl.program_id(1) == 0)
    def _():
        h_scr[...] = _rms(x_ref[...], g_ref[...]).astype(BF16)

    o_ref[...] = _dot(h_scr[...], w_ref[...]).astype(o_ref.dtype)


def _norm_matmul(x, g, w, *, tm, tn, out_dtype):
    m, k = x.shape
    n = w.shape[1]
    return pl.pallas_call(
        _norm_matmul_body,
        grid=(m // tm, n // tn),
        in_specs=[
            pl.BlockSpec((tm, k), lambda i, j: (i, 0)),
            pl.BlockSpec((1, k), lambda i, j: (0, 0)),
            pl.BlockSpec((k, tn), lambda i, j: (0, j)),
        ],
        out_specs=pl.BlockSpec((tm, tn), lambda i, j: (i, j)),
        out_shape=jax.ShapeDtypeStruct((m, n), out_dtype),
        scratch_shapes=[pltpu.VMEM((tm, k), BF16)],
        compiler_params=_params("parallel", "arbitrary"),
        name="norm_matmul",
    )(x, g.reshape(1, k), w)


def _inproj_body(x_ref, g_ref, w_ref, wlow_ref, wup_ref, bup_ref, o_ref, la_ref, h_scr):
    @pl.when(pl.program_id(1) == 0)
    def _():
        h = _rms(x_ref[...], g_ref[...]).astype(BF16)
        h_scr[...] = h
        low = _dot(h, wlow_ref[...])
        pre = _dot(low.astype(BF16), wup_ref[...]) + bup_ref[...]
        la_ref[...] = _log_sigmoid(pre) / GLA_GATE_TAU

    o_ref[...] = _dot(h_scr[...], w_ref[...]).astype(o_ref.dtype)


def _inproj(x, g, w_main, w_low, w_up, b_up, *, tm, tn):
    m, k = x.shape
    n = w_main.shape[1]
    return pl.pallas_call(
        _inproj_body,
        grid=(m // tm, n // tn),
        in_specs=[
            pl.BlockSpec((tm, k), lambda i, j: (i, 0)),
            pl.BlockSpec((1, k), lambda i, j: (0, 0)),
            pl.BlockSpec((k, tn), lambda i, j: (0, j)),
            pl.BlockSpec((k, LANES), lambda i, j: (0, 0)),
            pl.BlockSpec((LANES, GLA_K_WIDTH), lambda i, j: (0, 0)),
            pl.BlockSpec((1, GLA_K_WIDTH), lambda i, j: (0, 0)),
        ],
        out_specs=[
            pl.BlockSpec((tm, tn), lambda i, j: (i, j)),
            pl.BlockSpec((tm, GLA_K_WIDTH), lambda i, j: (i, 0)),
        ],
        out_shape=[
            jax.ShapeDtypeStruct((m, n), BF16),
            jax.ShapeDtypeStruct((m, GLA_K_WIDTH), F32),
        ],
        scratch_shapes=[pltpu.VMEM((tm, k), BF16)],
        compiler_params=_params("parallel", "arbitrary"),
        name="inproj",
    )(x, g.reshape(1, k), w_main, w_low, w_up, b_up.reshape(1, -1))


def _sb_body(q_ref, k_ref, v_ref, g_ref, o_ref, acc_ref, carry_ref, *, tq, heads):
    qi = pl.program_id(2)
    scale = SB_HEAD_DIM ** -0.5
    row = lax.broadcasted_iota(I32, (tq, tq), 0)
    col = lax.broadcasted_iota(I32, (tq, tq), 1)
    causal = col < row
    later = (row > col).astype(BF16)

    def tile(kj, diag):
        off = pl.multiple_of(kj * tq, tq)
        top = None
        for h in range(heads):
            cols = slice(h * SB_HEAD_DIM, (h + 1) * SB_HEAD_DIM)
            kblk = k_ref[0, pl.ds(off, tq), cols]
            vblk = v_ref[0, pl.ds(off, tq), cols]
            z = lax.dot_general(q_ref[0, :, cols], kblk, _NT, preferred_element_type=F32) * scale
            lp = jnp.log1p(jnp.exp(-jnp.abs(z)))
            log_beta = jnp.minimum(z, 0.0) - lp
            log_1m = log_beta - z
            if diag:
                log_1m = jnp.where(causal, log_1m, 0.0)
            hi, lo = _split_bf16(log_1m)
            between = _dot(hi, later) + _dot(lo, later)
            carry = carry_ref[h]
            a = jnp.exp(log_beta + between + carry)
            if diag:
                a = jnp.where(causal, a, 0.0)
            acc_ref[:, cols] += _dot(a.astype(BF16), vblk)
            carry = carry + jnp.sum(log_1m, axis=-1, keepdims=True)
            carry_ref[h] = carry
            top = jnp.max(carry) if top is None else jnp.maximum(top, jnp.max(carry))
        return top

    acc_ref[...] = jnp.zeros_like(acc_ref)
    carry_ref[...] = jnp.zeros_like(carry_ref)
    top = tile(qi, True)

    def cond(state):
        kj, top = state
        return jnp.logical_and(kj >= 0, top > SB_ZERO_LOG)

    def body(state):
        kj, _ = state
        return kj - 1, tile(kj, False)

    lax.while_loop(cond, body, (qi - 1, top))
    for h in range(heads):
        cols = slice(h * SB_HEAD_DIM, (h + 1) * SB_HEAD_DIM)
        o_ref[0, :, cols] = _rms(acc_ref[:, cols], g_ref[:, cols]).astype(o_ref.dtype)


def _sb_attention(proj, g, *, tq, heads):
    b, s, _ = proj.shape
    width = heads * SB_HEAD_DIM
    blk = lambda first: first // heads
    return pl.pallas_call(
        functools.partial(_sb_body, tq=tq, heads=heads),
        grid=(b, SB_HEADS // heads, s // tq),
        in_specs=[
            pl.BlockSpec((1, tq, width), lambda bi, h, qi: (bi, qi, blk(_SBQ_BLK) + h)),
            pl.BlockSpec((1, s, width), lambda bi, h, qi: (bi, 0, blk(_SBK_BLK) + h)),
            pl.BlockSpec((1, s, width), lambda bi, h, qi: (bi, 0, blk(_SBV_BLK) + h)),
            pl.BlockSpec((1, width), lambda bi, h, qi: (0, h)),
        ],
        out_specs=pl.BlockSpec((1, tq, width), lambda bi, h, qi: (bi, qi, h)),
        out_shape=jax.ShapeDtypeStruct((b, s, SB_WIDTH), BF16),
        scratch_shapes=[pltpu.VMEM((tq, width), F32), pltpu.VMEM((heads, tq, 1), F32)],
        compiler_params=_params("parallel", "parallel", "arbitrary"),
        name="sb_attention",
    )(proj, proj, proj, g.reshape(1, SB_WIDTH))


def _gla_body(q_ref, k_ref, v_ref, r_ref, la_ref, g_ref, o_ref, st_ref, *, rows):
    @pl.when(pl.program_id(1) == 0)
    def _():
        st_ref[...] = jnp.zeros_like(st_ref)

    shift = GLA_CHUNK.bit_length() - 1
    ri = lax.broadcasted_iota(I32, (rows, rows), 0)
    ci = lax.broadcasted_iota(I32, (rows, rows), 1)
    same = jnp.right_shift(ri, shift) == jnp.right_shift(ci, shift)
    tril = jnp.logical_and(same, ci <= ri)
    tril_b = tril.astype(BF16)
    same_b = same.astype(BF16)

    for h in range(GLA_HEADS):
        kc = slice(h * GLA_K_DIM, (h + 1) * GLA_K_DIM)
        vc = slice(h * GLA_V_DIM, (h + 1) * GLA_V_DIM)
        hi, lo = _split_bf16(la_ref[0, :, kc])
        gcum = _dot(tril_b, hi) + _dot(tril_b, lo)
        glast = _dot(same_b, hi) + _dot(same_b, lo)

        q = q_ref[0, :, kc].astype(F32) * (GLA_K_DIM ** -0.5)
        k = k_ref[0, :, kc].astype(F32)
        v = v_ref[0, :, vc]
        q_in = (q * jnp.exp(gcum)).astype(BF16)
        k_in = (k * jnp.exp(-gcum)).astype(BF16)
        k_out = (k * jnp.exp(glast - gcum)).astype(BF16)
        scores = lax.dot_general(q_in, k_in, _NT, preferred_element_type=F32)
        scores = jnp.where(tril, scores, 0.0)
        o_intra = _dot(scores.astype(BF16), v)

        state_t = st_ref[h]
        outs = []
        for n in range(rows // GLA_CHUNK):
            sl = slice(n * GLA_CHUNK, (n + 1) * GLA_CHUNK)
            o_inter = lax.dot_general(q_in[sl], state_t.astype(BF16), _NT, preferred_element_type=F32)
            outs.append(o_intra[sl] + o_inter)
            kv_t = lax.dot_general(v[sl], k_out[sl], _TN, preferred_element_type=F32)
            decay = jnp.exp(glast[n * GLA_CHUNK:n * GLA_CHUNK + 1, :])
            state_t = state_t * decay + kv_t
        st_ref[h] = state_t

        o = jnp.concatenate(outs, axis=0)
        gate = r_ref[0, :, vc].astype(F32)
        y = _rms(o, g_ref[:, vc]) * (gate * jax.nn.sigmoid(gate))
        o_ref[0, :, vc] = y.astype(o_ref.dtype)


def _gla(proj, log_a, g, *, rows):
    b, s, _ = proj.shape
    return pl.pallas_call(
        functools.partial(_gla_body, rows=rows),
        grid=(b, s // rows),
        in_specs=[
            pl.BlockSpec((1, rows, GLA_K_WIDTH), lambda bi, r: (bi, r, _GQ_BLK * LANES // GLA_K_WIDTH)),
            pl.BlockSpec((1, rows, GLA_K_WIDTH), lambda bi, r: (bi, r, _GK_BLK * LANES // GLA_K_WIDTH)),
            pl.BlockSpec((1, rows, GLA_V_WIDTH), lambda bi, r: (bi, r, _GV_BLK256 * GLA_V_DIM // GLA_V_WIDTH)),
            pl.BlockSpec((1, rows, GLA_V_WIDTH), lambda bi, r: (bi, r, _GR_BLK256 * GLA_V_DIM // GLA_V_WIDTH)),
            pl.BlockSpec((1, rows, GLA_K_WIDTH), lambda bi, r: (bi, r, 0)),
            pl.BlockSpec((1, GLA_V_WIDTH), lambda bi, r: (0, 0)),
        ],
        out_specs=pl.BlockSpec((1, rows, GLA_V_WIDTH), lambda bi, r: (bi, r, 0)),
        out_shape=jax.ShapeDtypeStruct((b, s, GLA_V_WIDTH), BF16),
        scratch_shapes=[pltpu.VMEM((GLA_HEADS, GLA_V_DIM, GLA_K_DIM), F32)],
        compiler_params=_params("parallel", "arbitrary"),
        name="gla",
    )(proj, proj, proj, proj, log_a, g.reshape(1, GLA_V_WIDTH))


def _mix_out_body(a_ref, b_ref, wa_ref, wb_ref, x_ref, o_ref):
    o_ref[...] = x_ref[...] + _dot(a_ref[...], wa_ref[...]) + _dot(b_ref[...], wb_ref[...])


def _mix_out(sb_o, gla_o, w_out, x, *, tm, tn):
    m, n = x.shape
    ka, kb = sb_o.shape[1], gla_o.shape[1]
    assert ka == kb
    return pl.pallas_call(
        _mix_out_body,
        grid=(m // tm, n // tn),
        in_specs=[
            pl.BlockSpec((tm, ka), lambda i, j: (i, 0)),
            pl.BlockSpec((tm, kb), lambda i, j: (i, 0)),
            pl.BlockSpec((ka, tn), lambda i, j: (0, j)),
            pl.BlockSpec((kb, tn), lambda i, j: (1, j)),
            pl.BlockSpec((tm, tn), lambda i, j: (i, j)),
        ],
        out_specs=pl.BlockSpec((tm, tn), lambda i, j: (i, j)),
        out_shape=jax.ShapeDtypeStruct((m, n), F32),
        compiler_params=_params("parallel", "arbitrary"),
        name="mix_out",
    )(sb_o, gla_o, w_out, w_out, x)


def _cross_body(q_ref, kv_ref, w_ref, x_ref, o_ref, a_scr):
    @pl.when(pl.program_id(1) == 0)
    def _():
        for h in range(CROSS_HEADS):
            lo, hi = h * CROSS_HEAD_DIM, (h + 1) * CROSS_HEAD_DIM
            q = q_ref[:, lo:hi]
            k = kv_ref[0, :, lo:hi]
            v = kv_ref[0, :, D_MODEL + lo:D_MODEL + hi]
            s = lax.dot_general(q, k, _NT, preferred_element_type=F32) * (CROSS_HEAD_DIM ** -0.5)
            e = jnp.exp(s - jnp.max(s, axis=-1, keepdims=True))
            p = e / jnp.sum(e, axis=-1, keepdims=True)
            a_scr[:, lo:hi] = _dot(p.astype(BF16), v).astype(BF16)

    o_ref[...] = x_ref[...] + _dot(a_scr[...], w_ref[...])


def _cross_attention(q, kv, w_co, x, *, seq, tm, tn):
    m, n = x.shape
    mem_len = kv.shape[1]
    tiles_per_batch = seq // tm
    return pl.pallas_call(
        _cross_body,
        grid=(m // tm, n // tn),
        in_specs=[
            pl.BlockSpec((tm, n), lambda i, j: (i, 0)),
            pl.BlockSpec((1, mem_len, 2 * n), lambda i, j: (i // tiles_per_batch, 0, 0)),
            pl.BlockSpec((n, tn), lambda i, j: (0, j)),
            pl.BlockSpec((tm, tn), lambda i, j: (i, j)),
        ],
        out_specs=pl.BlockSpec((tm, tn), lambda i, j: (i, j)),
        out_shape=jax.ShapeDtypeStruct((m, n), F32),
        scratch_shapes=[pltpu.VMEM((tm, n), BF16)],
        compiler_params=_params("parallel", "arbitrary"),
        name="cross_attention",
    )(q, kv, w_co, x)


def _router_body(x_ref, g_ref, w_ref, b_ref, idx_ref, wt_ref):
    h_hi, h_lo = _split_bf16(_rms(x_ref[...], g_ref[...]))
    logits = _dot(h_hi, w_ref[0]) + _dot(h_lo, w_ref[0]) + _dot(h_hi, w_ref[1]) + b_ref[...]
    lane = lax.broadcasted_iota(I32, logits.shape, 1)
    lane_f = lane.astype(F32)
    idx_out = jnp.zeros(logits.shape, F32)
    tops = []
    for k in range(TOP_K):
        m = jnp.max(logits, axis=-1, keepdims=True)
        pick = jnp.min(jnp.where(logits == m, lane_f, float(LANES)), axis=-1, keepdims=True)
        idx_out = jnp.where(lane == k, pick, idx_out)
        tops.append(m)
        logits = jnp.where(lane_f == pick, -jnp.inf, logits)
    es = [jnp.exp(t - tops[0]) for t in tops]
    inv = 1.0 / functools.reduce(lambda a, c: a + c, es)
    wt = jnp.zeros(logits.shape, F32)
    for k in range(TOP_K):
        wt = jnp.where(lane == k, es[k] * inv, wt)
    idx_ref[...] = idx_out.astype(I32)
    wt_ref[...] = wt


def _router(x, g, w_split, b_pad, *, tm):
    m, k = x.shape
    return pl.pallas_call(
        _router_body,
        grid=(m // tm,),
        in_specs=[
            pl.BlockSpec((tm, k), lambda i: (i, 0)),
            pl.BlockSpec((1, k), lambda i: (0, 0)),
            pl.BlockSpec((2, k, LANES), lambda i: (0, 0, 0)),
            pl.BlockSpec((1, LANES), lambda i: (0, 0)),
        ],
        out_specs=[
            pl.BlockSpec((tm, LANES), lambda i: (i, 0)),
            pl.BlockSpec((tm, LANES), lambda i: (i, 0)),
        ],
        out_shape=[
            jax.ShapeDtypeStruct((m, LANES), I32),
            jax.ShapeDtypeStruct((m, LANES), F32),
        ],
        compiler_params=_params("parallel"),
        name="router",
    )(x, g.reshape(1, k), w_split, b_pad)


def _row_slice(t, pitch=SEGS):
    return pl.ds(pl.multiple_of(t * pitch, 8), SEGS)


def _col_block(s, n_tokens, first_token=0, pitch=SEGS):
    return pl.ds(first_token * pitch + s, n_tokens, stride=pitch)


def _stage_indices(idx_hbm, idx_smem, sem, tile):
    n = idx_smem.shape[0]
    cp = pltpu.make_async_copy(idx_hbm.at[pl.ds(pl.multiple_of(tile * n, n), n)], idx_smem, sem)
    cp.start()
    cp.wait()


def _start_rows(n, row_copy):
    def start_pair(i, c):
        row_copy(2 * i).start(priority=0)
        row_copy(2 * i + 1).start(priority=1)
        return c

    lax.fori_loop(0, n // 2, start_pair, 0, unroll=4)


def _dispatch_body(dest_hbm, x_ref, g_ref, rows_hbm, idx_smem, stage, sem_idx, sem):
    _stage_indices(dest_hbm, idx_smem, sem_idx, pl.program_id(0))
    h = _rms(x_ref[...], g_ref[...])
    tm = h.shape[0]
    for s in range(SEGS):
        stage[_col_block(s, tm, pitch=ROW_PITCH), :] = h[:, s * LANES:(s + 1) * LANES]
    shift = TOP_K.bit_length() - 1

    def row_copy(a):
        return pltpu.make_async_copy(stage.at[_row_slice(jnp.right_shift(a, shift), ROW_PITCH)],
                                     rows_hbm.at[_row_slice(idx_smem[a])], sem)

    n = idx_smem.shape[0]
    _start_rows(n, row_copy)
    pltpu.make_async_copy(rows_hbm.at[pl.ds(0, n * SEGS)], rows_hbm.at[pl.ds(n * SEGS, n * SEGS)], sem).wait()


def _dispatch(dest, x, g, n_slots, *, tm):
    m, d = x.shape
    return pl.pallas_call(
        _dispatch_body,
        grid_spec=pltpu.PrefetchScalarGridSpec(
            num_scalar_prefetch=0,
            grid=(m // tm,),
            in_specs=[
                pl.BlockSpec(memory_space=pl.ANY),
                pl.BlockSpec((tm, d), lambda i: (i, 0)),
                pl.BlockSpec((1, d), lambda i: (0, 0)),
            ],
            out_specs=pl.BlockSpec(memory_space=pl.ANY),
            scratch_shapes=[
                pltpu.SMEM((TOP_K * tm,), I32),
                pltpu.VMEM((tm * ROW_PITCH, LANES), F32),
                pltpu.SemaphoreType.DMA,
                pltpu.SemaphoreType.DMA,
            ],
        ),
        out_shape=jax.ShapeDtypeStruct((n_slots * SEGS, LANES), F32),
        compiler_params=_params("arbitrary"),
        name="moe_dispatch",
    )(dest, x, g.reshape(1, d))


def _experts_body(blk_ref, exp_ref, rows_ref, x_ref, wg_ref, wl_ref, bg_ref, bl_ref, wd_ref, bd_ref, y_hbm,
                  xs_scr, acc_scr, y_stage, sem):
    c = pl.program_id(0)
    j = pl.program_id(1)
    last_c = pl.num_programs(0) - 1
    last_j = pl.num_programs(1) - 1

    def live_subs(chunk):
        return lax.div(rows_ref[chunk] + (MOE_SUB - 1), MOE_SUB)

    rows = rows_ref[c]

    def y_copy(chunk, r):
        first = (blk_ref[chunk] * MOE_CHUNK + r * MOE_SUB) * SEGS
        return pltpu.make_async_copy(y_stage.at[pl.ds(r * MOE_SUB * SEGS, MOE_SUB * SEGS)],
                                     y_hbm.at[pl.ds(pl.multiple_of(first, MOE_SUB * SEGS), MOE_SUB * SEGS)], sem)

    def for_live_subs(chunk, fn):
        n_live = live_subs(chunk)
        for r in range(MOE_CHUNK // MOE_SUB):
            @pl.when(r < n_live)
            def _():
                fn(y_copy(chunk, r))

    @pl.when(jnp.logical_and(j == 0, rows > 0))
    def _():
        for r in range(MOE_CHUNK // MOE_SUB):
            live = r * MOE_SUB + lax.broadcasted_iota(I32, (MOE_SUB, LANES), 0) < rows
            for s in range(SEGS):
                seg = jnp.where(live, x_ref[_col_block(s, MOE_SUB, r * MOE_SUB), :], 0.0)
                xs_scr[r * MOE_SUB:(r + 1) * MOE_SUB, s * LANES:(s + 1) * LANES] = seg.astype(BF16)

    def hidden():
        xs = xs_scr[...]
        gate = jnp.minimum(_dot(xs, wg_ref[0].astype(BF16)) + bg_ref[0], SWIGLU_LIMIT)
        lin = jnp.clip(_dot(xs, wl_ref[0].astype(BF16)) + bl_ref[0], -SWIGLU_LIMIT, SWIGLU_LIMIT)
        return (gate * jax.nn.sigmoid(SWIGLU_ALPHA * gate) * (lin + 1.0)).astype(BF16)

    def down_tiles(act):
        for n in range(acc_scr.shape[1] // MXU_WIDTH):
            cols = slice(n * MXU_WIDTH, (n + 1) * MXU_WIDTH)
            yield cols, _dot(act, wd_ref[0, :, cols].astype(BF16))

    @pl.when(jnp.logical_and(j == 0, rows > 0))
    def _():
        for cols, out in down_tiles(hidden()):
            acc_scr[:, cols] = bd_ref[0, :, cols] + out

    @pl.when(jnp.logical_and(jnp.logical_and(j > 0, j < last_j), rows > 0))
    def _():
        for cols, out in down_tiles(hidden()):
            acc_scr[:, cols] += out

    @pl.when(jnp.logical_and(j == last_j, rows > 0))
    def _():
        @pl.when(c > 0)
        def _():
            for_live_subs(c - 1, lambda cp: cp.wait())

        for cols, out in down_tiles(hidden()):
            y = acc_scr[:, cols] + out
            for i in range(MXU_WIDTH // LANES):
                s = cols.start // LANES + i
                y_stage[_col_block(s, MOE_CHUNK), :] = y[:, i * LANES:(i + 1) * LANES]
        for_live_subs(c, lambda cp: cp.start())


    @pl.when(jnp.logical_and(j == last_j, c == last_c))
    def _():
        for_live_subs(blk_ref[c], lambda cp: cp.wait())


def _experts(chunk_blk, chunk_expert, chunk_rows, x_rows, w_gate_up, b_gate_up, w_down, b_down):
    n_chunks = chunk_blk.shape[0]
    d = SEGS * LANES
    n_j = D_FF // MOE_TN
    return pl.pallas_call(
        _experts_body,
        grid_spec=pltpu.PrefetchScalarGridSpec(
            num_scalar_prefetch=3,
            grid=(n_chunks, n_j),
            in_specs=[
                pl.BlockSpec((MOE_CHUNK * SEGS, LANES), lambda c, j, blk, ex, rw: (blk[c], 0),
                             pipeline_mode=pl.Buffered(1)),
                pl.BlockSpec((1, d, MOE_TN), lambda c, j, blk, ex, rw: (ex[c], 0, j)),
                pl.BlockSpec((1, d, MOE_TN), lambda c, j, blk, ex, rw: (ex[c], 0, n_j + j)),
                pl.BlockSpec((1, 1, MOE_TN), lambda c, j, blk, ex, rw: (ex[c], 0, j)),
                pl.BlockSpec((1, 1, MOE_TN), lambda c, j, blk, ex, rw: (ex[c], 0, n_j + j)),
                pl.BlockSpec((1, MOE_TN, d), lambda c, j, blk, ex, rw: (ex[c], j, 0)),
                pl.BlockSpec((1, 1, d), lambda c, j, blk, ex, rw: (ex[c], 0, 0)),
            ],
            out_specs=pl.BlockSpec(memory_space=pl.ANY),
            scratch_shapes=[
                pltpu.VMEM((MOE_CHUNK, d), BF16),
                pltpu.VMEM((MOE_CHUNK, d), F32),
                pltpu.VMEM((MOE_CHUNK * SEGS, LANES), F32),
                pltpu.SemaphoreType.DMA,
            ],
        ),
        out_shape=jax.ShapeDtypeStruct((n_chunks * MOE_CHUNK * SEGS, LANES), F32),
        compiler_params=_params("arbitrary", "arbitrary"),
        name="moe_experts",
    )(chunk_blk, chunk_expert, chunk_rows, x_rows, w_gate_up, w_gate_up,
      b_gate_up.reshape(N_EXPERTS, 1, -1), b_gate_up.reshape(N_EXPERTS, 1, -1),
      w_down, b_down.reshape(N_EXPERTS, 1, -1))


def _combine_body(dest_hbm, x_ref, wt_ref, g_ref, rows_hbm, o_ref, idx_smem, ybuf, sem_idx, sems, *, final_norm):
    i = pl.program_id(0)
    n_tiles = pl.num_programs(0)
    shift = TOP_K.bit_length() - 1
    tm = x_ref.shape[0]
    n = idx_smem.shape[0]

    def gather(tile, slot):
        _stage_indices(dest_hbm, idx_smem, sem_idx, tile)

        def row_copy(a):
            row = jnp.bitwise_and(a, TOP_K - 1) * tm + jnp.right_shift(a, shift)
            return pltpu.make_async_copy(rows_hbm.at[_row_slice(idx_smem[a])],
                                         ybuf.at[slot, _row_slice(row, ROW_PITCH)], sems.at[slot])

        _start_rows(n, row_copy)

    def combine(slot):
        if slot == 0:
            @pl.when(i == 0)
            def _():
                gather(0, 0)

        @pl.when(i + 1 < n_tiles)
        def _():
            gather(i + 1, 1 - slot)

        pltpu.make_async_copy(rows_hbm.at[pl.ds(0, n * SEGS)], ybuf.at[slot, pl.ds(0, n * SEGS)], sems.at[slot]).wait()

        wt = wt_ref[...]
        segs = []
        for s in range(SEGS):
            acc = x_ref[:, s * LANES:(s + 1) * LANES]
            for k in range(TOP_K):
                acc = acc + ybuf[slot, _col_block(s, tm, k * tm, ROW_PITCH), :] * wt[:, k:k + 1]
            segs.append(acc)
        if final_norm:
            ssq = functools.reduce(lambda a, c: a + c, [jnp.sum(v * v, axis=-1, keepdims=True) for v in segs])
            inv = lax.rsqrt(ssq / (SEGS * LANES) + EPS)
            segs = [v * inv * g_ref[:, s * LANES:(s + 1) * LANES] for s, v in enumerate(segs)]
        for s, v in enumerate(segs):
            o_ref[:, s * LANES:(s + 1) * LANES] = v

    for slot in range(2):
        @pl.when(jnp.bitwise_and(i, 1) == slot)
        def _():
            combine(slot)


def _combine(dest, x, wt, g, y_rows, *, tm, final_norm):
    m, d = x.shape
    return pl.pallas_call(
        functools.partial(_combine_body, final_norm=final_norm),
        grid_spec=pltpu.PrefetchScalarGridSpec(
            num_scalar_prefetch=0,
            grid=(m // tm,),
            in_specs=[
                pl.BlockSpec(memory_space=pl.ANY),
                pl.BlockSpec((tm, d), lambda i: (i, 0)),
                pl.BlockSpec((tm, LANES), lambda i: (i, 0)),
                pl.BlockSpec((1, d), lambda i: (0, 0)),
                pl.BlockSpec(memory_space=pl.ANY),
            ],
            out_specs=pl.BlockSpec((tm, d), lambda i: (i, 0)),
            scratch_shapes=[
                pltpu.SMEM((TOP_K * tm,), I32),
                pltpu.VMEM((2, TOP_K * tm * ROW_PITCH, LANES), F32),
                pltpu.SemaphoreType.DMA,
                pltpu.SemaphoreType.DMA((2,)),
            ],
        ),
        out_shape=jax.ShapeDtypeStruct((m, d), F32),
        compiler_params=_params("arbitrary"),
        name="moe_combine",
    )(dest, x, wt, g.reshape(1, d), y_rows)


def _routing_tables(top_idx, n_chunks):
    expert_flat = top_idx.reshape(-1)
    onehot = (expert_flat[:, None] == jnp.arange(N_EXPERTS, dtype=I32)[None, :]).astype(I32)
    running = jnp.cumsum(onehot, axis=0)
    rank = jnp.sum(onehot * running, axis=1) - 1
    counts = running[-1]
    chunks_per_expert = (counts + MOE_CHUNK - 1) // MOE_CHUNK
    chunk_end = jnp.cumsum(chunks_per_expert)
    chunk_start = chunk_end - chunks_per_expert
    dest = (chunk_start * MOE_CHUNK)[expert_flat] + rank

    c = jnp.arange(n_chunks, dtype=I32)
    used = chunk_end[-1]
    c_eff = jnp.minimum(c, used - 1)
    chunk_expert = jnp.sum((chunk_end[None, :] <= c_eff[:, None]).astype(I32), axis=1)
    rows = jnp.clip(counts[chunk_expert] - (c_eff - chunk_start[chunk_expert]) * MOE_CHUNK, 0, MOE_CHUNK)
    chunk_rows = jnp.where(c < used, rows, 0).astype(I32)
    return dest.astype(I32), c_eff.astype(I32), chunk_expert, chunk_rows


def kernel(x, mem, norm_mix_g, w_in, w_gla_gate_up, b_gla_gate, sb_norm_g, gla_norm_g, w_out,
           norm_cross_g, norm_mem_g, w_cq, w_ckv, w_co,
           norm_moe_g, w_router, b_router, w_gate_up, b_gate_up, w_down, b_down,
           norm_final_g):
    batch, seq, d = x.shape
    n_tok = batch * seq
    depth = w_in.shape[0]
    xf = x.reshape(n_tok, d)
    mem_f = mem.reshape(-1, d)
    mem_len = mem.shape[1]

    for layer in range(depth):
        w_in_l = w_in[layer]
        w_main = w_in_l[:, :PROJ_MAIN].astype(BF16)
        w_low = jnp.pad(w_in_l[:, PROJ_MAIN:], ((0, 0), (0, LANES - GLA_GATE_RANK))).astype(BF16)
        w_up = jnp.pad(w_gla_gate_up[layer], ((0, LANES - GLA_GATE_RANK), (0, 0))).astype(BF16)
        proj, log_a = _inproj(xf, norm_mix_g[layer], w_main, w_low, w_up, b_gla_gate[layer], tm=1024, tn=1024)
        proj = proj.reshape(batch, seq, PROJ_MAIN)
        sb_o = _sb_attention(proj, sb_norm_g[layer], tq=256, heads=4)
        gla_o = _gla(proj, log_a.reshape(batch, seq, GLA_K_WIDTH), gla_norm_g[layer], rows=256)
        xf = _mix_out(sb_o.reshape(n_tok, SB_WIDTH), gla_o.reshape(n_tok, GLA_V_WIDTH),
                      w_out[layer].astype(BF16), xf, tm=1024, tn=1024)

        q = _norm_matmul(xf, norm_cross_g[layer], w_cq[layer].astype(BF16), tm=1024, tn=1024, out_dtype=BF16)
        kv = _norm_matmul(mem_f, norm_mem_g[layer], w_ckv[layer].astype(BF16), tm=1024, tn=1024, out_dtype=BF16)
        xf = _cross_attention(q, kv.reshape(batch, mem_len, 2 * d), w_co[layer].astype(BF16), xf,
                              seq=seq, tm=1024, tn=1024)

        w_r = jnp.pad(w_router[layer], ((0, 0), (0, LANES - N_EXPERTS)))
        w_r_hi = w_r.astype(BF16)
        w_r_lo = (w_r - w_r_hi.astype(F32)).astype(BF16)
        b_r = jnp.pad(b_router[layer], (0, LANES - N_EXPERTS), constant_values=-1e30).reshape(1, LANES)
        top_idx, top_w = _router(xf, norm_moe_g[layer], jnp.stack([w_r_hi, w_r_lo]), b_r, tm=512)

        n_chunks = n_tok * TOP_K // MOE_CHUNK + N_EXPERTS
        dest, chunk_blk, chunk_expert, chunk_rows = _routing_tables(top_idx[:, :TOP_K], n_chunks)
        x_rows = _dispatch(dest, xf, norm_moe_g[layer], n_chunks * MOE_CHUNK, tm=512)
        y_rows = _experts(chunk_blk, chunk_expert, chunk_rows, x_rows,
                          w_gate_up[layer], b_gate_up[layer], w_down[layer], b_down[layer])
        xf = _combine(dest, xf, top_w, norm_final_g, y_rows, tm=256, final_norm=layer + 1 == depth)

    return xf.reshape(batch, seq, d)
```

```python
import functools

import jax
import jax.numpy as jnp
from jax import lax
from jax.experimental import pallas as pl
from jax.experimental.pallas import tpu as pltpu

F32 = jnp.float32
BF16 = jnp.bfloat16
I32 = jnp.int32

D_MODEL = 2048
SB_HEADS = 8
SB_HEAD_DIM = 128
SB_WIDTH = SB_HEADS * SB_HEAD_DIM
GLA_HEADS = 4
GLA_K_DIM = 128
GLA_V_DIM = 256
GLA_K_WIDTH = GLA_HEADS * GLA_K_DIM
GLA_V_WIDTH = GLA_HEADS * GLA_V_DIM
GLA_GATE_RANK = 16
GLA_GATE_TAU = 16.0
GLA_CHUNK = 64
PROJ_MAIN = 3 * SB_WIDTH + 2 * GLA_K_WIDTH + 2 * GLA_V_WIDTH
CROSS_HEADS = 4
CROSS_HEAD_DIM = D_MODEL // CROSS_HEADS
N_EXPERTS = 32
TOP_K = 4
D_FF = D_MODEL
SWIGLU_LIMIT = 7.0
SWIGLU_ALPHA = 1.702
EPS = 1e-6

LANES = 128
MXU_WIDTH = 256
VMEM_BYTES = 64 * 1024 * 1024
VMEM_LIMIT_BYTES = VMEM_BYTES - 8 * 1024 * 1024

SEGS = D_MODEL // LANES
ROW_PITCH = 24

_SBQ_BLK = 0
_SBK_BLK = SB_WIDTH // LANES
_SBV_BLK = 2 * SB_WIDTH // LANES
_GQ_BLK = 3 * SB_WIDTH // LANES
_GK_BLK = _GQ_BLK + GLA_K_WIDTH // LANES
_GV_BLK256 = (3 * SB_WIDTH + 2 * GLA_K_WIDTH) // GLA_V_DIM
_GR_BLK256 = _GV_BLK256 + GLA_V_WIDTH // GLA_V_DIM

SB_ZERO_LOG = -104.0

MOE_CHUNK = 1152
MOE_SUB = 384
MOE_TN = 256

_NT = (((1,), (1,)), ((), ()))
_TN = (((0,), (0,)), ((), ()))


def _params(*sem):
    return pltpu.CompilerParams(dimension_semantics=sem, vmem_limit_bytes=VMEM_LIMIT_BYTES)


def _rms(x, g):
    return x * lax.rsqrt(jnp.mean(x * x, axis=-1, keepdims=True) + EPS) * g


def _log_sigmoid(x):
    return -(jnp.maximum(-x, 0.0) + jnp.log1p(jnp.exp(-jnp.abs(x))))


def _split_bf16(x):
    hi = x.astype(BF16)
    lo = (x - hi.astype(F32)).astype(BF16)
    return hi, lo


def _dot(a, b):
    return jnp.dot(a, b, preferred_element_type=F32)


def _norm_matmul_body(x_ref, g_ref, w_ref, o_ref, h_scr):
    @pl.when(pl.program_id(1) == 0)
    def _():
        h_scr[...] = _rms(x_ref[...], g_ref[...]).astype(BF16)

    o_ref[...] = _dot(h_scr[...], w_ref[...]).astype(o_ref.dtype)


def _norm_matmul(x, g, w, *, tm, tn, out_dtype):
    m, k = x.shape
    n = w.shape[1]
    return pl.pallas_call(
        _norm_matmul_body,
        grid=(m // tm, n // tn),
        in_specs=[
            pl.BlockSpec((tm, k), lambda i, j: (i, 0)),
            pl.BlockSpec((1, k), lambda i, j: (0, 0)),
            pl.BlockSpec((k, tn), lambda i, j: (0, j)),
        ],
        out_specs=pl.BlockSpec((tm, tn), lambda i, j: (i, j)),
        out_shape=jax.ShapeDtypeStruct((m, n), out_dtype),
        scratch_shapes=[pltpu.VMEM((tm, k), BF16)],
        compiler_params=_params("parallel", "arbitrary"),
        name="norm_matmul",
    )(x, g.reshape(1, k), w)


def _inproj_body(x_ref, g_ref, w_ref, wlow_ref, wup_ref, bup_ref, o_ref, la_ref, h_scr):
    @pl.when(pl.program_id(1) == 0)
    def _():
        h = _rms(x_ref[...], g_ref[...]).astype(BF16)
        h_scr[...] = h
        low = _dot(h, wlow_ref[...])
        pre = _dot(low.astype(BF16), wup_ref[...]) + bup_ref[...]
        la_ref[...] = _log_sigmoid(pre) / GLA_GATE_TAU

    o_ref[...] = _dot(h_scr[...], w_ref[...]).astype(o_ref.dtype)


def _inproj(x, g, w_main, w_low, w_up, b_up, *, tm, tn):
    m, k = x.shape
    n = w_main.shape[1]
    return pl.pallas_call(
        _inproj_body,
        grid=(m // tm, n // tn),
        in_specs=[
            pl.BlockSpec((tm, k), lambda i, j: (i, 0)),
            pl.BlockSpec((1, k), lambda i, j: (0, 0)),
            pl.BlockSpec((k, tn), lambda i, j: (0, j)),
            pl.BlockSpec((k, LANES), lambda i, j: (0, 0)),
            pl.BlockSpec((LANES, GLA_K_WIDTH), lambda i, j: (0, 0)),
            pl.BlockSpec((1, GLA_K_WIDTH), lambda i, j: (0, 0)),
        ],
        out_specs=[
            pl.BlockSpec((tm, tn), lambda i, j: (i, j)),
            pl.BlockSpec((tm, GLA_K_WIDTH), lambda i, j: (i, 0)),
        ],
        out_shape=[
            jax.ShapeDtypeStruct((m, n), BF16),
            jax.ShapeDtypeStruct((m, GLA_K_WIDTH), F32),
        ],
        scratch_shapes=[pltpu.VMEM((tm, k), BF16)],
        compiler_params=_params("parallel", "arbitrary"),
        name="inproj",
    )(x, g.reshape(1, k), w_main, w_low, w_up, b_up.reshape(1, -1))


def _sb_body(q_ref, k_ref, v_ref, g_ref, o_ref, acc_ref, carry_ref, *, tq, heads):
    qi = pl.program_id(2)
    scale = SB_HEAD_DIM ** -0.5
    row = lax.broadcasted_iota(I32, (tq, tq), 0)
    col = lax.broadcasted_iota(I32, (tq, tq), 1)
    causal = col < row
    later = (row > col).astype(BF16)

    def tile(kj, diag):
        off = pl.multiple_of(kj * tq, tq)
        top = None
        for h in range(heads):
            cols = slice(h * SB_HEAD_DIM, (h + 1) * SB_HEAD_DIM)
            kblk = k_ref[0, pl.ds(off, tq), cols]
            vblk = v_ref[0, pl.ds(off, tq), cols]
            z = lax.dot_general(q_ref[0, :, cols], kblk, _NT, preferred_element_type=F32) * scale
            lp = jnp.log1p(jnp.exp(-jnp.abs(z)))
            log_beta = jnp.minimum(z, 0.0) - lp
            log_1m = log_beta - z
            if diag:
                log_1m = jnp.where(causal, log_1m, 0.0)
            hi, lo = _split_bf16(log_1m)
            between = _dot(hi, later) + _dot(lo, later)
            carry = carry_ref[h]
            a = jnp.exp(log_beta + between + carry)
            if diag:
                a = jnp.where(causal, a, 0.0)
            acc_ref[:, cols] += _dot(a.astype(BF16), vblk)
            carry = carry + jnp.sum(log_1m, axis=-1, keepdims=True)
            carry_ref[h] = carry
            top = jnp.max(carry) if top is None else jnp.maximum(top, jnp.max(carry))
        return top

    acc_ref[...] = jnp.zeros_like(acc_ref)
    carry_ref[...] = jnp.zeros_like(carry_ref)
    top = tile(qi, True)

    def cond(state):
        kj, top = state
        return jnp.logical_and(kj >= 0, top > SB_ZERO_LOG)

    def body(state):
        kj, _ = state
        return kj - 1, tile(kj, False)

    lax.while_loop(cond, body, (qi - 1, top))
    for h in range(heads):
        cols = slice(h * SB_HEAD_DIM, (h + 1) * SB_HEAD_DIM)
        o_ref[0, :, cols] = _rms(acc_ref[:, cols], g_ref[:, cols]).astype(o_ref.dtype)


def _sb_attention(proj, g, *, tq, heads):
    b, s, _ = proj.shape
    width = heads * SB_HEAD_DIM
    blk = lambda first: first // heads
    return pl.pallas_call(
        functools.partial(_sb_body, tq=tq, heads=heads),
        grid=(b, SB_HEADS // heads, s // tq),
        in_specs=[
            pl.BlockSpec((1, tq, width), lambda bi, h, qi: (bi, qi, blk(_SBQ_BLK) + h)),
            pl.BlockSpec((1, s, width), lambda bi, h, qi: (bi, 0, blk(_SBK_BLK) + h)),
            pl.BlockSpec((1, s, width), lambda bi, h, qi: (bi, 0, blk(_SBV_BLK) + h)),
            pl.BlockSpec((1, width), lambda bi, h, qi: (0, h)),
        ],
        out_specs=pl.BlockSpec((1, tq, width), lambda bi, h, qi: (bi, qi, h)),
        out_shape=jax.ShapeDtypeStruct((b, s, SB_WIDTH), BF16),
        scratch_shapes=[pltpu.VMEM((tq, width), F32), pltpu.VMEM((heads, tq, 1), F32)],
        compiler_params=_params("parallel", "parallel", "arbitrary"),
        name="sb_attention",
    )(proj, proj, proj, g.reshape(1, SB_WIDTH))


def _gla_body(q_ref, k_ref, v_ref, r_ref, la_ref, g_ref, o_ref, st_ref, *, rows):
    @pl.when(pl.program_id(1) == 0)
    def _():
        st_ref[...] = jnp.zeros_like(st_ref)

    shift = GLA_CHUNK.bit_length() - 1
    ri = lax.broadcasted_iota(I32, (rows, rows), 0)
    ci = lax.broadcasted_iota(I32, (rows, rows), 1)
    same = jnp.right_shift(ri, shift) == jnp.right_shift(ci, shift)
    tril = jnp.logical_and(same, ci <= ri)
    tril_b = tril.astype(BF16)
    same_b = same.astype(BF16)

    for h in range(GLA_HEADS):
        kc = slice(h * GLA_K_DIM, (h + 1) * GLA_K_DIM)
        vc = slice(h * GLA_V_DIM, (h + 1) * GLA_V_DIM)
        hi, lo = _split_bf16(la_ref[0, :, kc])
        gcum = _dot(tril_b, hi) + _dot(tril_b, lo)
        glast = _dot(same_b, hi) + _dot(same_b, lo)

        q = q_ref[0, :, kc].astype(F32) * (GLA_K_DIM ** -0.5)
        k = k_ref[0, :, kc].astype(F32)
        v = v_ref[0, :, vc]
        q_in = (q * jnp.exp(gcum)).astype(BF16)
        k_in = (k * jnp.exp(-gcum)).astype(BF16)
        k_out = (k * jnp.exp(glast - gcum)).astype(BF16)
        scores = lax.dot_general(q_in, k_in, _NT, preferred_element_type=F32)
        scores = jnp.where(tril, scores, 0.0)
        o_intra = _dot(scores.astype(BF16), v)

        state_t = st_ref[h]
        outs = []
        for n in range(rows // GLA_CHUNK):
            sl = slice(n * GLA_CHUNK, (n + 1) * GLA_CHUNK)
            o_inter = lax.dot_general(q_in[sl], state_t.astype(BF16), _NT, preferred_element_type=F32)
            outs.append(o_intra[sl] + o_inter)
            kv_t = lax.dot_general(v[sl], k_out[sl], _TN, preferred_element_type=F32)
            decay = jnp.exp(glast[n * GLA_CHUNK:n * GLA_CHUNK + 1, :])
            state_t = state_t * decay + kv_t
        st_ref[h] = state_t

        o = jnp.concatenate(outs, axis=0)
        gate = r_ref[0, :, vc].astype(F32)
        y = _rms(o, g_ref[:, vc]) * (gate * jax.nn.sigmoid(gate))
        o_ref[0, :, vc] = y.astype(o_ref.dtype)


def _gla(proj, log_a, g, *, rows):
    b, s, _ = proj.shape
    return pl.pallas_call(
        functools.partial(_gla_body, rows=rows),
        grid=(b, s // rows),
        in_specs=[
            pl.BlockSpec((1, rows, GLA_K_WIDTH), lambda bi, r: (bi, r, _GQ_BLK * LANES // GLA_K_WIDTH)),
            pl.BlockSpec((1, rows, GLA_K_WIDTH), lambda bi, r: (bi, r, _GK_BLK * LANES // GLA_K_WIDTH)),
            pl.BlockSpec((1, rows, GLA_V_WIDTH), lambda bi, r: (bi, r, _GV_BLK256 * GLA_V_DIM // GLA_V_WIDTH)),
            pl.BlockSpec((1, rows, GLA_V_WIDTH), lambda bi, r: (bi, r, _GR_BLK256 * GLA_V_DIM // GLA_V_WIDTH)),
            pl.BlockSpec((1, rows, GLA_K_WIDTH), lambda bi, r: (bi, r, 0)),
            pl.BlockSpec((1, GLA_V_WIDTH), lambda bi, r: (0, 0)),
        ],
        out_specs=pl.BlockSpec((1, rows, GLA_V_WIDTH), lambda bi, r: (bi, r, 0)),
        out_shape=jax.ShapeDtypeStruct((b, s, GLA_V_WIDTH), BF16),
        scratch_shapes=[pltpu.VMEM((GLA_HEADS, GLA_V_DIM, GLA_K_DIM), F32)],
        compiler_params=_params("parallel", "arbitrary"),
        name="gla",
    )(proj, proj, proj, proj, log_a, g.reshape(1, GLA_V_WIDTH))


def _mix_out_body(a_ref, b_ref, wa_ref, wb_ref, x_ref, o_ref):
    o_ref[...] = x_ref[...] + _dot(a_ref[...], wa_ref[...]) + _dot(b_ref[...], wb_ref[...])


def _mix_out(sb_o, gla_o, w_out, x, *, tm, tn):
    m, n = x.shape
    ka, kb = sb_o.shape[1], gla_o.shape[1]
    assert ka == kb
    return pl.pallas_call(
        _mix_out_body,
        grid=(m // tm, n // tn),
        in_specs=[
            pl.BlockSpec((tm, ka), lambda i, j: (i, 0)),
            pl.BlockSpec((tm, kb), lambda i, j: (i, 0)),
            pl.BlockSpec((ka, tn), lambda i, j: (0, j)),
            pl.BlockSpec((kb, tn), lambda i, j: (1, j)),
            pl.BlockSpec((tm, tn), lambda i, j: (i, j)),
        ],
        out_specs=pl.BlockSpec((tm, tn), lambda i, j: (i, j)),
        out_shape=jax.ShapeDtypeStruct((m, n), F32),
        compiler_params=_params("parallel", "arbitrary"),
        name="mix_out",
    )(sb_o, gla_o, w_out, w_out, x)


def _cross_body(q_ref, kv_ref, w_ref, x_ref, o_ref, a_scr):
    @pl.when(pl.program_id(1) == 0)
    def _():
        for h in range(CROSS_HEADS):
            lo, hi = h * CROSS_HEAD_DIM, (h + 1) * CROSS_HEAD_DIM
            q = q_ref[:, lo:hi]
            k = kv_ref[0, :, lo:hi]
            v = kv_ref[0, :, D_MODEL + lo:D_MODEL + hi]
            s = lax.dot_general(q, k, _NT, preferred_element_type=F32) * (CROSS_HEAD_DIM ** -0.5)
            e = jnp.exp(s - jnp.max(s, axis=-1, keepdims=True))
            p = e / jnp.sum(e, axis=-1, keepdims=True)
            a_scr[:, lo:hi] = _dot(p.astype(BF16), v).astype(BF16)

    o_ref[...] = x_ref[...] + _dot(a_scr[...], w_ref[...])


def _cross_attention(q, kv, w_co, x, *, seq, tm, tn):
    m, n = x.shape
    mem_len = kv.shape[1]
    tiles_per_batch = seq // tm
    return pl.pallas_call(
        _cross_body,
        grid=(m // tm, n // tn),
        in_specs=[
            pl.BlockSpec((tm, n), lambda i, j: (i, 0)),
            pl.BlockSpec((1, mem_len, 2 * n), lambda i, j: (i // tiles_per_batch, 0, 0)),
            pl.BlockSpec((n, tn), lambda i, j: (0, j)),
            pl.BlockSpec((tm, tn), lambda i, j: (i, j)),
        ],
        out_specs=pl.BlockSpec((tm, tn), lambda i, j: (i, j)),
        out_shape=jax.ShapeDtypeStruct((m, n), F32),
        scratch_shapes=[pltpu.VMEM((tm, n), BF16)],
        compiler_params=_params("parallel", "arbitrary"),
        name="cross_attention",
    )(q, kv, w_co, x)


def _router_body(x_ref, g_ref, w_ref, b_ref, idx_ref, wt_ref):
    h_hi, h_lo = _split_bf16(_rms(x_ref[...], g_ref[...]))
    logits = _dot(h_hi, w_ref[0]) + _dot(h_lo, w_ref[0]) + _dot(h_hi, w_ref[1]) + b_ref[...]
    lane = lax.broadcasted_iota(I32, logits.shape, 1)
    lane_f = lane.astype(F32)
    idx_out = jnp.zeros(logits.shape, F32)
    tops = []
    for k in range(TOP_K):
        m = jnp.max(logits, axis=-1, keepdims=True)
        pick = jnp.min(jnp.where(logits == m, lane_f, float(LANES)), axis=-1, keepdims=True)
        idx_out = jnp.where(lane == k, pick, idx_out)
        tops.append(m)
        logits = jnp.where(lane_f == pick, -jnp.inf, logits)
    es = [jnp.exp(t - tops[0]) for t in tops]
    inv = 1.0 / functools.reduce(lambda a, c: a + c, es)
    wt = jnp.zeros(logits.shape, F32)
    for k in range(TOP_K):
        wt = jnp.where(lane == k, es[k] * inv, wt)
    idx_ref[...] = idx_out.astype(I32)
    wt_ref[...] = wt


def _router(x, g, w_split, b_pad, *, tm):
    m, k = x.shape
    return pl.pallas_call(
        _router_body,
        grid=(m // tm,),
        in_specs=[
            pl.BlockSpec((tm, k), lambda i: (i, 0)),
            pl.BlockSpec((1, k), lambda i: (0, 0)),
            pl.BlockSpec((2, k, LANES), lambda i: (0, 0, 0)),
            pl.BlockSpec((1, LANES), lambda i: (0, 0)),
        ],
        out_specs=[
            pl.BlockSpec((tm, LANES), lambda i: (i, 0)),
            pl.BlockSpec((tm, LANES), lambda i: (i, 0)),
        ],
        out_shape=[
            jax.ShapeDtypeStruct((m, LANES), I32),
            jax.ShapeDtypeStruct((m, LANES), F32),
        ],
        compiler_params=_params("parallel"),
        name="router",
    )(x, g.reshape(1, k), w_split, b_pad)


def _row_slice(t, pitch=SEGS):
    return pl.ds(pl.multiple_of(t * pitch, 8), SEGS)


def _col_block(s, n_tokens, first_token=0, pitch=SEGS):
    return pl.ds(first_token * pitch + s, n_tokens, stride=pitch)


def _stage_indices(idx_hbm, idx_smem, sem, tile):
    n = idx_smem.shape[0]
    cp = pltpu.make_async_copy(idx_hbm.at[pl.ds(pl.multiple_of(tile * n, n), n)], idx_smem, sem)
    cp.start()
    cp.wait()


def _start_rows(n, row_copy):
    def start_pair(i, c):
        row_copy(2 * i).start(priority=0)
        row_copy(2 * i + 1).start(priority=1)
        return c

    lax.fori_loop(0, n // 2, start_pair, 0, unroll=4)


def _dispatch_body(dest_hbm, x_ref, g_ref, rows_hbm, idx_smem, stage, sem_idx, sem):
    _stage_indices(dest_hbm, idx_smem, sem_idx, pl.program_id(0))
    h = _rms(x_ref[...], g_ref[...])
    tm = h.shape[0]
    for s in range(SEGS):
        stage[_col_block(s, tm, pitch=ROW_PITCH), :] = h[:, s * LANES:(s + 1) * LANES]
    shift = TOP_K.bit_length() - 1

    def row_copy(a):
        return pltpu.make_async_copy(stage.at[_row_slice(jnp.right_shift(a, shift), ROW_PITCH)],
                                     rows_hbm.at[_row_slice(idx_smem[a])], sem)

    n = idx_smem.shape[0]
    _start_rows(n, row_copy)
    pltpu.make_async_copy(rows_hbm.at[pl.ds(0, n * SEGS)], rows_hbm.at[pl.ds(n * SEGS, n * SEGS)], sem).wait()


def _dispatch(dest, x, g, n_slots, *, tm):
    m, d = x.shape
    return pl.pallas_call(
        _dispatch_body,
        grid_spec=pltpu.PrefetchScalarGridSpec(
            num_scalar_prefetch=0,
            grid=(m // tm,),
            in_specs=[
                pl.BlockSpec(memory_space=pl.ANY),
                pl.BlockSpec((tm, d), lambda i: (i, 0)),
                pl.BlockSpec((1, d), lambda i: (0, 0)),
            ],
            out_specs=pl.BlockSpec(memory_space=pl.ANY),
            scratch_shapes=[
                pltpu.SMEM((TOP_K * tm,), I32),
                pltpu.VMEM((tm * ROW_PITCH, LANES), F32),
                pltpu.SemaphoreType.DMA,
                pltpu.SemaphoreType.DMA,
            ],
        ),
        out_shape=jax.ShapeDtypeStruct((n_slots * SEGS, LANES), F32),
        compiler_params=_params("arbitrary"),
        name="moe_dispatch",
    )(dest, x, g.reshape(1, d))


def _experts_body(blk_ref, exp_ref, rows_ref, x_ref, wg_ref, wl_ref, bg_ref, bl_ref, wd_ref, bd_ref, y_hbm,
                  xs_scr, acc_scr, y_stage, sem):
    c = pl.program_id(0)
    j = pl.program_id(1)
    last_c = pl.num_programs(0) - 1
    last_j = pl.num_programs(1) - 1

    def live_subs(chunk):
        return lax.div(rows_ref[chunk] + (MOE_SUB - 1), MOE_SUB)

    rows = rows_ref[c]

    def y_copy(chunk, r):
        first = (blk_ref[chunk] * MOE_CHUNK + r * MOE_SUB) * SEGS
        return pltpu.make_async_copy(y_stage.at[pl.ds(r * MOE_SUB * SEGS, MOE_SUB * SEGS)],
                                     y_hbm.at[pl.ds(pl.multiple_of(first, MOE_SUB * SEGS), MOE_SUB * SEGS)], sem)

    def for_live_subs(chunk, fn):
        n_live = live_subs(chunk)
        for r in range(MOE_CHUNK // MOE_SUB):
            @pl.when(r < n_live)
            def _():
                fn(y_copy(chunk, r))

    @pl.when(jnp.logical_and(j == 0, rows > 0))
    def _():
        for r in range(MOE_CHUNK // MOE_SUB):
            live = r * MOE_SUB + lax.broadcasted_iota(I32, (MOE_SUB, LANES), 0) < rows
            for s in range(SEGS):
                seg = jnp.where(live, x_ref[_col_block(s, MOE_SUB, r * MOE_SUB), :], 0.0)
                xs_scr[r * MOE_SUB:(r + 1) * MOE_SUB, s * LANES:(s + 1) * LANES] = seg.astype(BF16)

    def hidden():
        xs = xs_scr[...]
        gate = jnp.minimum(_dot(xs, wg_ref[0].astype(BF16)) + bg_ref[0], SWIGLU_LIMIT)
        lin = jnp.clip(_dot(xs, wl_ref[0].astype(BF16)) + bl_ref[0], -SWIGLU_LIMIT, SWIGLU_LIMIT)
        return (gate * jax.nn.sigmoid(SWIGLU_ALPHA * gate) * (lin + 1.0)).astype(BF16)

    def down_tiles(act):
        for n in range(acc_scr.shape[1] // MXU_WIDTH):
            cols = slice(n * MXU_WIDTH, (n + 1) * MXU_WIDTH)
            yield cols, _dot(act, wd_ref[0, :, cols].astype(BF16))

    @pl.when(jnp.logical_and(j == 0, rows > 0))
    def _():
        for cols, out in down_tiles(hidden()):
            acc_scr[:, cols] = bd_ref[0, :, cols] + out

    @pl.when(jnp.logical_and(jnp.logical_and(j > 0, j < last_j), rows > 0))
    def _():
        for cols, out in down_tiles(hidden()):
            acc_scr[:, cols] += out

    @pl.when(jnp.logical_and(j == last_j, rows > 0))
    def _():
        @pl.when(c > 0)
        def _():
            for_live_subs(c - 1, lambda cp: cp.wait())

        for cols, out in down_tiles(hidden()):
            y = acc_scr[:, cols] + out
            for i in range(MXU_WIDTH // LANES):
                s = cols.start // LANES + i
                y_stage[_col_block(s, MOE_CHUNK), :] = y[:, i * LANES:(i + 1) * LANES]
        for_live_subs(c, lambda cp: cp.start())


    @pl.when(jnp.logical_and(j == last_j, c == last_c))
    def _():
        for_live_subs(blk_ref[c], lambda cp: cp.wait())


def _experts(chunk_blk, chunk_expert, chunk_rows, x_rows, w_gate_up, b_gate_up, w_down, b_down):
    n_chunks = chunk_blk.shape[0]
    d = SEGS * LANES
    n_j = D_FF // MOE_TN
    return pl.pallas_call(
        _experts_body,
        grid_spec=pltpu.PrefetchScalarGridSpec(
            num_scalar_prefetch=3,
            grid=(n_chunks, n_j),
            in_specs=[
                pl.BlockSpec((MOE_CHUNK * SEGS, LANES), lambda c, j, blk, ex, rw: (blk[c], 0)),
                pl.BlockSpec((1, d, MOE_TN), lambda c, j, blk, ex, rw: (ex[c], 0, j)),
                pl.BlockSpec((1, d, MOE_TN), lambda c, j, blk, ex, rw: (ex[c], 0, n_j + j)),
                pl.BlockSpec((1, 1, MOE_TN), lambda c, j, blk, ex, rw: (ex[c], 0, j)),
                pl.BlockSpec((1, 1, MOE_TN), lambda c, j, blk, ex, rw: (ex[c], 0, n_j + j)),
                pl.BlockSpec((1, MOE_TN, d), lambda c, j, blk, ex, rw: (ex[c], j, 0)),
                pl.BlockSpec((1, 1, d), lambda c, j, blk, ex, rw: (ex[c], 0, 0)),
            ],
            out_specs=pl.BlockSpec(memory_space=pl.ANY),
            scratch_shapes=[
                pltpu.VMEM((MOE_CHUNK, d), BF16),
                pltpu.VMEM((MOE_CHUNK, d), F32),
                pltpu.VMEM((MOE_CHUNK * SEGS, LANES), F32),
                pltpu.SemaphoreType.DMA,
            ],
        ),
        out_shape=jax.ShapeDtypeStruct((n_chunks * MOE_CHUNK * SEGS, LANES), F32),
        compiler_params=_params("arbitrary", "arbitrary"),
        name="moe_experts",
    )(chunk_blk, chunk_expert, chunk_rows, x_rows, w_gate_up, w_gate_up,
      b_gate_up.reshape(N_EXPERTS, 1, -1), b_gate_up.reshape(N_EXPERTS, 1, -1),
      w_down, b_down.reshape(N_EXPERTS, 1, -1))


def _combine_body(dest_hbm, x_ref, wt_ref, g_ref, rows_hbm, o_ref, idx_smem, ybuf, sem_idx, sems, *, final_norm):
    i = pl.program_id(0)
    n_tiles = pl.num_programs(0)
    shift = TOP_K.bit_length() - 1
    tm = x_ref.shape[0]
    n = idx_smem.shape[0]

    def gather(tile, slot):
        _stage_indices(dest_hbm, idx_smem, sem_idx, tile)

        def row_copy(a):
            row = jnp.bitwise_and(a, TOP_K - 1) * tm + jnp.right_shift(a, shift)
            return pltpu.make_async_copy(rows_hbm.at[_row_slice(idx_smem[a])],
                                         ybuf.at[slot, _row_slice(row, ROW_PITCH)], sems.at[slot])

        _start_rows(n, row_copy)

    def combine(slot):
        if slot == 0:
            @pl.when(i == 0)
            def _():
                gather(0, 0)

        @pl.when(i + 1 < n_tiles)
        def _():
            gather(i + 1, 1 - slot)

        pltpu.make_async_copy(rows_hbm.at[pl.ds(0, n * SEGS)], ybuf.at[slot, pl.ds(0, n * SEGS)], sems.at[slot]).wait()

        wt = wt_ref[...]
        segs = []
        for s in range(SEGS):
            acc = x_ref[:, s * LANES:(s + 1) * LANES]
            for k in range(TOP_K):
                acc = acc + ybuf[slot, _col_block(s, tm, k * tm, ROW_PITCH), :] * wt[:, k:k + 1]
            segs.append(acc)
        if final_norm:
            ssq = functools.reduce(lambda a, c: a + c, [jnp.sum(v * v, axis=-1, keepdims=True) for v in segs])
            inv = lax.rsqrt(ssq / (SEGS * LANES) + EPS)
            segs = [v * inv * g_ref[:, s * LANES:(s + 1) * LANES] for s, v in enumerate(segs)]
        for s, v in enumerate(segs):
            o_ref[:, s * LANES:(s + 1) * LANES] = v

    for slot in range(2):
        @pl.when(jnp.bitwise_and(i, 1) == slot)
        def _():
            combine(slot)


def _combine(dest, x, wt, g, y_rows, *, tm, final_norm):
    m, d = x.shape
    return pl.pallas_call(
        functools.partial(_combine_body, final_norm=final_norm),
        grid_spec=pltpu.PrefetchScalarGridSpec(
            num_scalar_prefetch=0,
            grid=(m // tm,),
            in_specs=[
                pl.BlockSpec(memory_space=pl.ANY),
                pl.BlockSpec((tm, d), lambda i: (i, 0)),
                pl.BlockSpec((tm, LANES), lambda i: (i, 0)),
                pl.BlockSpec((1, d), lambda i: (0, 0)),
                pl.BlockSpec(memory_space=pl.ANY),
            ],
            out_specs=pl.BlockSpec((tm, d), lambda i: (i, 0)),
            scratch_shapes=[
                pltpu.SMEM((TOP_K * tm,), I32),
                pltpu.VMEM((2, TOP_K * tm * ROW_PITCH, LANES), F32),
                pltpu.SemaphoreType.DMA,
                pltpu.SemaphoreType.DMA((2,)),
            ],
        ),
        out_shape=jax.ShapeDtypeStruct((m, d), F32),
        compiler_params=_params("arbitrary"),
        name="moe_combine",
    )(dest, x, wt, g.reshape(1, d), y_rows)


def _routing_tables(top_idx, n_chunks):
    expert_flat = top_idx.reshape(-1)
    onehot = (expert_flat[:, None] == jnp.arange(N_EXPERTS, dtype=I32)[None, :]).astype(I32)
    running = jnp.cumsum(onehot, axis=0)
    rank = jnp.sum(onehot * running, axis=1) - 1
    counts = running[-1]
    chunks_per_expert = (counts + MOE_CHUNK - 1) // MOE_CHUNK
    chunk_end = jnp.cumsum(chunks_per_expert)
    chunk_start = chunk_end - chunks_per_expert
    dest = (chunk_start * MOE_CHUNK)[expert_flat] + rank

    c = jnp.arange(n_chunks, dtype=I32)
    used = chunk_end[-1]
    c_eff = jnp.minimum(c, used - 1)
    chunk_expert = jnp.sum((chunk_end[None, :] <= c_eff[:, None]).astype(I32), axis=1)
    rows = jnp.clip(counts[chunk_expert] - (c_eff - chunk_start[chunk_expert]) * MOE_CHUNK, 0, MOE_CHUNK)
    chunk_rows = jnp.where(c < used, rows, 0).astype(I32)
    return dest.astype(I32), c_eff.astype(I32), chunk_expert, chunk_rows


def kernel(x, mem, norm_mix_g, w_in, w_gla_gate_up, b_gla_gate, sb_norm_g, gla_norm_g, w_out,
           norm_cross_g, norm_mem_g, w_cq, w_ckv, w_co,
           norm_moe_g, w_router, b_router, w_gate_up, b_gate_up, w_down, b_down,
           norm_final_g):
    batch, seq, d = x.shape
    n_tok = batch * seq
    depth = w_in.shape[0]
    xf = x.reshape(n_tok, d)
    mem_f = mem.reshape(-1, d)
    mem_len = mem.shape[1]

    for layer in range(depth):
        w_in_l = w_in[layer]
        w_main = w_in_l[:, :PROJ_MAIN].astype(BF16)
        w_low = jnp.pad(w_in_l[:, PROJ_MAIN:], ((0, 0), (0, LANES - GLA_GATE_RANK))).astype(BF16)
        w_up = jnp.pad(w_gla_gate_up[layer], ((0, LANES - GLA_GATE_RANK), (0, 0))).astype(BF16)
        proj, log_a = _inproj(xf, norm_mix_g[layer], w_main, w_low, w_up, b_gla_gate[layer], tm=1024, tn=1024)
        proj = proj.reshape(batch, seq, PROJ_MAIN)
        sb_o = _sb_attention(proj, sb_norm_g[layer], tq=256, heads=4)
        gla_o = _gla(proj, log_a.reshape(batch, seq, GLA_K_WIDTH), gla_norm_g[layer], rows=256)
        xf = _mix_out(sb_o.reshape(n_tok, SB_WIDTH), gla_o.reshape(n_tok, GLA_V_WIDTH),
                      w_out[layer].astype(BF16), xf, tm=1024, tn=1024)

        q = _norm_matmul(xf, norm_cross_g[layer], w_cq[layer].astype(BF16), tm=1024, tn=1024, out_dtype=BF16)
        kv = _norm_matmul(mem_f, norm_mem_g[layer], w_ckv[layer].astype(BF16), tm=1024, tn=1024, out_dtype=BF16)
        xf = _cross_attention(q, kv.reshape(batch, mem_len, 2 * d), w_co[layer].astype(BF16), xf,
                              seq=seq, tm=1024, tn=1024)

        w_r = jnp.pad(w_router[layer], ((0, 0), (0, LANES - N_EXPERTS)))
        w_r_hi = w_r.astype(BF16)
        w_r_lo = (w_r - w_r_hi.astype(F32)).astype(BF16)
        b_r = jnp.pad(b_router[layer], (0, LANES - N_EXPERTS), constant_values=-1e30).reshape(1, LANES)
        top_idx, top_w = _router(xf, norm_moe_g[layer], jnp.stack([w_r_hi, w_r_lo]), b_r, tm=512)

        n_chunks = n_tok * TOP_K // MOE_CHUNK + N_EXPERTS
        dest, chunk_blk, chunk_expert, chunk_rows = _routing_tables(top_idx[:, :TOP_K], n_chunks)
        x_rows = _dispatch(dest, xf, norm_moe_g[layer], n_chunks * MOE_CHUNK, tm=512)
        y_rows = _experts(chunk_blk, chunk_expert, chunk_rows, x_rows,
                          w_gate_up[layer], b_gate_up[layer], w_down[layer], b_down[layer])
        xf = _combine(dest, xf, top_w, norm_final_g, y_rows, tm=256, final_norm=layer + 1 == depth)

    return xf.reshape(batch, seq, d)
```

```python
import functools

import jax
import jax.numpy as jnp
from jax import lax
from jax.experimental import pallas as pl
from jax.experimental.pallas import tpu as pltpu

F32 = jnp.float32
BF16 = jnp.bfloat16
I32 = jnp.int32

D_MODEL = 2048
SB_HEADS = 8
SB_HEAD_DIM = 128
SB_WIDTH = SB_HEADS * SB_HEAD_DIM
GLA_HEADS = 4
GLA_K_DIM = 128
GLA_V_DIM = 256
GLA_K_WIDTH = GLA_HEADS * GLA_K_DIM
GLA_V_WIDTH = GLA_HEADS * GLA_V_DIM
GLA_GATE_RANK = 16
GLA_GATE_TAU = 16.0
GLA_CHUNK = 64
PROJ_MAIN = 3 * SB_WIDTH + 2 * GLA_K_WIDTH + 2 * GLA_V_WIDTH
CROSS_HEADS = 4
CROSS_HEAD_DIM = D_MODEL // CROSS_HEADS
N_EXPERTS = 32
TOP_K = 4
D_FF = D_MODEL
SWIGLU_LIMIT = 7.0
SWIGLU_ALPHA = 1.702
EPS = 1e-6

LANES = 128
MXU_WIDTH = 256
VMEM_BYTES = 64 * 1024 * 1024
VMEM_LIMIT_BYTES = VMEM_BYTES - 8 * 1024 * 1024

SEGS = D_MODEL // LANES
ROW_PITCH = 24

_SBQ_BLK = 0
_SBK_BLK = SB_WIDTH // LANES
_SBV_BLK = 2 * SB_WIDTH // LANES
_GQ_BLK = 3 * SB_WIDTH // LANES
_GK_BLK = _GQ_BLK + GLA_K_WIDTH // LANES
_GV_BLK256 = (3 * SB_WIDTH + 2 * GLA_K_WIDTH) // GLA_V_DIM
_GR_BLK256 = _GV_BLK256 + GLA_V_WIDTH // GLA_V_DIM

SB_ZERO_LOG = -104.0

MOE_CHUNK = 1152
MOE_SUB = 384
MOE_TN = 256

_NT = (((1,), (1,)), ((), ()))
_TN = (((0,), (0,)), ((), ()))


def _params(*sem):
    return pltpu.CompilerParams(dimension_semantics=sem, vmem_limit_bytes=VMEM_LIMIT_BYTES)


def _rms(x, g):
    return x * lax.rsqrt(jnp.mean(x * x, axis=-1, keepdims=True) + EPS) * g


def _log_sigmoid(x):
    return -(jnp.maximum(-x, 0.0) + jnp.log1p(jnp.exp(-jnp.abs(x))))


def _split_bf16(x):
    hi = x.astype(BF16)
    lo = (x - hi.astype(F32)).astype(BF16)
    return hi, lo


def _dot(a, b):
    return jnp.dot(a, b, preferred_element_type=F32)


def _norm_matmul_body(x_ref, g_ref, w_ref, o_ref, h_scr):
    @pl.when(pl.program_id(1) == 0)
    def _():
        h_scr[...] = _rms(x_ref[...], g_ref[...]).astype(BF16)

    o_ref[...] = _dot(h_scr[...], w_ref[...]).astype(o_ref.dtype)


def _norm_matmul(x, g, w, *, tm, tn, out_dtype):
    m, k = x.shape
    n = w.shape[1]
    return pl.pallas_call(
        _norm_matmul_body,
        grid=(m // tm, n // tn),
        in_specs=[
            pl.BlockSpec((tm, k), lambda i, j: (i, 0)),
            pl.BlockSpec((1, k), lambda i, j: (0, 0)),
            pl.BlockSpec((k, tn), lambda i, j: (0, j)),
        ],
        out_specs=pl.BlockSpec((tm, tn), lambda i, j: (i, j)),
        out_shape=jax.ShapeDtypeStruct((m, n), out_dtype),
        scratch_shapes=[pltpu.VMEM((tm, k), BF16)],
        compiler_params=_params("parallel", "arbitrary"),
        name="norm_matmul",
    )(x, g.reshape(1, k), w)


def _inproj_body(x_ref, g_ref, w_ref, wlow_ref, wup_ref, bup_ref, o_ref, la_ref, h_scr):
    @pl.when(pl.program_id(1) == 0)
    def _():
        h = _rms(x_ref[...], g_ref[...]).astype(BF16)
        h_scr[...] = h
        low = _dot(h, wlow_ref[...])
        pre = _dot(low.astype(BF16), wup_ref[...]) + bup_ref[...]
        la_ref[...] = _log_sigmoid(pre) / GLA_GATE_TAU

    o_ref[...] = _dot(h_scr[...], w_ref[...].astype(BF16)).astype(o_ref.dtype)


def _inproj(x, g, w_main, w_low, w_up, b_up, *, tm, tn):
    m, k = x.shape
    n = PROJ_MAIN
    return pl.pallas_call(
        _inproj_body,
        grid=(m // tm, n // tn),
        in_specs=[
            pl.BlockSpec((tm, k), lambda i, j: (i, 0)),
            pl.BlockSpec((1, k), lambda i, j: (0, 0)),
            pl.BlockSpec((k, tn), lambda i, j: (0, j)),
            pl.BlockSpec((k, LANES), lambda i, j: (0, 0)),
            pl.BlockSpec((LANES, GLA_K_WIDTH), lambda i, j: (0, 0)),
            pl.BlockSpec((1, GLA_K_WIDTH), lambda i, j: (0, 0)),
        ],
        out_specs=[
            pl.BlockSpec((tm, tn), lambda i, j: (i, j)),
            pl.BlockSpec((tm, GLA_K_WIDTH), lambda i, j: (i, 0)),
        ],
        out_shape=[
            jax.ShapeDtypeStruct((m, n), BF16),
            jax.ShapeDtypeStruct((m, GLA_K_WIDTH), F32),
        ],
        scratch_shapes=[pltpu.VMEM((tm, k), BF16)],
        compiler_params=_params("parallel", "arbitrary"),
        name="inproj",
    )(x, g.reshape(1, k), w_main, w_low, w_up, b_up.reshape(1, -1))


def _sb_body(q_ref, k_ref, v_ref, g_ref, o_ref, acc_ref, carry_ref, *, tq, heads):
    qi = pl.program_id(2)
    scale = SB_HEAD_DIM ** -0.5
    row = lax.broadcasted_iota(I32, (tq, tq), 0)
    col = lax.broadcasted_iota(I32, (tq, tq), 1)
    causal = col < row
    later = (row > col).astype(BF16)

    def tile(kj, diag):
        off = pl.multiple_of(kj * tq, tq)
        top = None
        for h in range(heads):
            cols = slice(h * SB_HEAD_DIM, (h + 1) * SB_HEAD_DIM)
            kblk = k_ref[0, pl.ds(off, tq), cols]
            vblk = v_ref[0, pl.ds(off, tq), cols]
            z = lax.dot_general(q_ref[0, :, cols], kblk, _NT, preferred_element_type=F32) * scale
            lp = jnp.log1p(jnp.exp(-jnp.abs(z)))
            log_beta = jnp.minimum(z, 0.0) - lp
            log_1m = log_beta - z
            if diag:
                log_1m = jnp.where(causal, log_1m, 0.0)
            hi, lo = _split_bf16(log_1m)
            between = _dot(hi, later) + _dot(lo, later)
            carry = carry_ref[h]
            a = jnp.exp(log_beta + between + carry)
            if diag:
                a = jnp.where(causal, a, 0.0)
            acc_ref[:, cols] += _dot(a.astype(BF16), vblk)
            carry = carry + jnp.sum(log_1m, axis=-1, keepdims=True)
            carry_ref[h] = carry
            top = jnp.max(carry) if top is None else jnp.maximum(top, jnp.max(carry))
        return top

    acc_ref[...] = jnp.zeros_like(acc_ref)
    carry_ref[...] = jnp.zeros_like(carry_ref)
    top = tile(qi, True)

    def cond(state):
        kj, top = state
        return jnp.logical_and(kj >= 0, top > SB_ZERO_LOG)

    def body(state):
        kj, _ = state
        return kj - 1, tile(kj, False)

    lax.while_loop(cond, body, (qi - 1, top))
    for h in range(heads):
        cols = slice(h * SB_HEAD_DIM, (h + 1) * SB_HEAD_DIM)
        o_ref[0, :, cols] = _rms(acc_ref[:, cols], g_ref[:, cols]).astype(o_ref.dtype)


def _sb_attention(proj, g, *, tq, heads):
    b, s, _ = proj.shape
    width = heads * SB_HEAD_DIM
    blk = lambda first: first // heads
    return pl.pallas_call(
        functools.partial(_sb_body, tq=tq, heads=heads),
        grid=(b, SB_HEADS // heads, s // tq),
        in_specs=[
            pl.BlockSpec((1, tq, width), lambda bi, h, qi: (bi, qi, blk(_SBQ_BLK) + h)),
            pl.BlockSpec((1, s, width), lambda bi, h, qi: (bi, 0, blk(_SBK_BLK) + h)),
            pl.BlockSpec((1, s, width), lambda bi, h, qi: (bi, 0, blk(_SBV_BLK) + h)),
            pl.BlockSpec((1, width), lambda bi, h, qi: (0, h)),
        ],
        out_specs=pl.BlockSpec((1, tq, width), lambda bi, h, qi: (bi, qi, h)),
        out_shape=jax.ShapeDtypeStruct((b, s, SB_WIDTH), BF16),
        scratch_shapes=[pltpu.VMEM((tq, width), F32), pltpu.VMEM((heads, tq, 1), F32)],
        compiler_params=_params("parallel", "parallel", "arbitrary"),
        name="sb_attention",
    )(proj, proj, proj, g.reshape(1, SB_WIDTH))


def _gla_body(q_ref, k_ref, v_ref, r_ref, la_ref, g_ref, o_ref, st_ref, *, rows):
    @pl.when(pl.program_id(1) == 0)
    def _():
        st_ref[...] = jnp.zeros_like(st_ref)

    shift = GLA_CHUNK.bit_length() - 1
    ri = lax.broadcasted_iota(I32, (rows, rows), 0)
    ci = lax.broadcasted_iota(I32, (rows, rows), 1)
    same = jnp.right_shift(ri, shift) == jnp.right_shift(ci, shift)
    tril = jnp.logical_and(same, ci <= ri)
    tril_b = tril.astype(BF16)
    same_b = same.astype(BF16)

    for h in range(GLA_HEADS):
        kc = slice(h * GLA_K_DIM, (h + 1) * GLA_K_DIM)
        vc = slice(h * GLA_V_DIM, (h + 1) * GLA_V_DIM)
        hi, lo = _split_bf16(la_ref[0, :, kc])
        gcum = _dot(tril_b, hi) + _dot(tril_b, lo)
        glast = _dot(same_b, hi) + _dot(same_b, lo)

        q = q_ref[0, :, kc].astype(F32) * (GLA_K_DIM ** -0.5)
        k = k_ref[0, :, kc].astype(F32)
        v = v_ref[0, :, vc]
        q_in = (q * jnp.exp(gcum)).astype(BF16)
        k_in = (k * jnp.exp(-gcum)).astype(BF16)
        k_out = (k * jnp.exp(glast - gcum)).astype(BF16)
        scores = lax.dot_general(q_in, k_in, _NT, preferred_element_type=F32)
        scores = jnp.where(tril, scores, 0.0)
        o_intra = _dot(scores.astype(BF16), v)

        state_t = st_ref[h]
        outs = []
        for n in range(rows // GLA_CHUNK):
            sl = slice(n * GLA_CHUNK, (n + 1) * GLA_CHUNK)
            o_inter = lax.dot_general(q_in[sl], state_t.astype(BF16), _NT, preferred_element_type=F32)
            outs.append(o_intra[sl] + o_inter)
            kv_t = lax.dot_general(v[sl], k_out[sl], _TN, preferred_element_type=F32)
            decay = jnp.exp(glast[n * GLA_CHUNK:n * GLA_CHUNK + 1, :])
            state_t = state_t * decay + kv_t
        st_ref[h] = state_t

        o = jnp.concatenate(outs, axis=0)
        gate = r_ref[0, :, vc].astype(F32)
        y = _rms(o, g_ref[:, vc]) * (gate * jax.nn.sigmoid(gate))
        o_ref[0, :, vc] = y.astype(o_ref.dtype)


def _gla(proj, log_a, g, *, rows):
    b, s, _ = proj.shape
    return pl.pallas_call(
        functools.partial(_gla_body, rows=rows),
        grid=(b, s // rows),
        in_specs=[
            pl.BlockSpec((1, rows, GLA_K_WIDTH), lambda bi, r: (bi, r, _GQ_BLK * LANES // GLA_K_WIDTH)),
            pl.BlockSpec((1, rows, GLA_K_WIDTH), lambda bi, r: (bi, r, _GK_BLK * LANES // GLA_K_WIDTH)),
            pl.BlockSpec((1, rows, GLA_V_WIDTH), lambda bi, r: (bi, r, _GV_BLK256 * GLA_V_DIM // GLA_V_WIDTH)),
            pl.BlockSpec((1, rows, GLA_V_WIDTH), lambda bi, r: (bi, r, _GR_BLK256 * GLA_V_DIM // GLA_V_WIDTH)),
            pl.BlockSpec((1, rows, GLA_K_WIDTH), lambda bi, r: (bi, r, 0)),
            pl.BlockSpec((1, GLA_V_WIDTH), lambda bi, r: (0, 0)),
        ],
        out_specs=pl.BlockSpec((1, rows, GLA_V_WIDTH), lambda bi, r: (bi, r, 0)),
        out_shape=jax.ShapeDtypeStruct((b, s, GLA_V_WIDTH), BF16),
        scratch_shapes=[pltpu.VMEM((GLA_HEADS, GLA_V_DIM, GLA_K_DIM), F32)],
        compiler_params=_params("parallel", "arbitrary"),
        name="gla",
    )(proj, proj, proj, proj, log_a, g.reshape(1, GLA_V_WIDTH))


def _mix_out_body(a_ref, b_ref, wa_ref, wb_ref, x_ref, o_ref):
    o_ref[...] = x_ref[...] + _dot(a_ref[...], wa_ref[...]) + _dot(b_ref[...], wb_ref[...])


def _mix_out(sb_o, gla_o, w_out, x, *, tm, tn):
    m, n = x.shape
    ka, kb = sb_o.shape[1], gla_o.shape[1]
    assert ka == kb
    return pl.pallas_call(
        _mix_out_body,
        grid=(m // tm, n // tn),
        in_specs=[
            pl.BlockSpec((tm, ka), lambda i, j: (i, 0)),
            pl.BlockSpec((tm, kb), lambda i, j: (i, 0)),
            pl.BlockSpec((ka, tn), lambda i, j: (0, j)),
            pl.BlockSpec((kb, tn), lambda i, j: (1, j)),
            pl.BlockSpec((tm, tn), lambda i, j: (i, j)),
        ],
        out_specs=pl.BlockSpec((tm, tn), lambda i, j: (i, j)),
        out_shape=jax.ShapeDtypeStruct((m, n), F32),
        compiler_params=_params("parallel", "arbitrary"),
        name="mix_out",
    )(sb_o, gla_o, w_out, w_out, x)


def _cross_body(q_ref, kv_ref, w_ref, x_ref, o_ref, a_scr):
    @pl.when(pl.program_id(1) == 0)
    def _():
        for h in range(CROSS_HEADS):
            lo, hi = h * CROSS_HEAD_DIM, (h + 1) * CROSS_HEAD_DIM
            q = q_ref[:, lo:hi]
            k = kv_ref[0, :, lo:hi]
            v = kv_ref[0, :, D_MODEL + lo:D_MODEL + hi]
            s = lax.dot_general(q, k, _NT, preferred_element_type=F32) * (CROSS_HEAD_DIM ** -0.5)
            e = jnp.exp(s - jnp.max(s, axis=-1, keepdims=True))
            p = e / jnp.sum(e, axis=-1, keepdims=True)
            a_scr[:, lo:hi] = _dot(p.astype(BF16), v).astype(BF16)

    o_ref[...] = x_ref[...] + _dot(a_scr[...], w_ref[...])


def _cross_attention(q, kv, w_co, x, *, seq, tm, tn):
    m, n = x.shape
    mem_len = kv.shape[1]
    tiles_per_batch = seq // tm
    return pl.pallas_call(
        _cross_body,
        grid=(m // tm, n // tn),
        in_specs=[
            pl.BlockSpec((tm, n), lambda i, j: (i, 0)),
            pl.BlockSpec((1, mem_len, 2 * n), lambda i, j: (i // tiles_per_batch, 0, 0)),
            pl.BlockSpec((n, tn), lambda i, j: (0, j)),
            pl.BlockSpec((tm, tn), lambda i, j: (i, j)),
        ],
        out_specs=pl.BlockSpec((tm, tn), lambda i, j: (i, j)),
        out_shape=jax.ShapeDtypeStruct((m, n), F32),
        scratch_shapes=[pltpu.VMEM((tm, n), BF16)],
        compiler_params=_params("parallel", "arbitrary"),
        name="cross_attention",
    )(q, kv, w_co, x)


def _router_body(x_ref, g_ref, w_ref, b_ref, idx_ref, wt_ref):
    h_hi, h_lo = _split_bf16(_rms(x_ref[...], g_ref[...]))
    logits = _dot(h_hi, w_ref[0]) + _dot(h_lo, w_ref[0]) + _dot(h_hi, w_ref[1]) + b_ref[...]
    lane = lax.broadcasted_iota(I32, logits.shape, 1)
    lane_f = lane.astype(F32)
    idx_out = jnp.zeros(logits.shape, F32)
    tops = []
    for k in range(TOP_K):
        m = jnp.max(logits, axis=-1, keepdims=True)
        pick = jnp.min(jnp.where(logits == m, lane_f, float(LANES)), axis=-1, keepdims=True)
        idx_out = jnp.where(lane == k, pick, idx_out)
        tops.append(m)
        logits = jnp.where(lane_f == pick, -jnp.inf, logits)
    es = [jnp.exp(t - tops[0]) for t in tops]
    inv = 1.0 / functools.reduce(lambda a, c: a + c, es)
    wt = jnp.zeros(logits.shape, F32)
    for k in range(TOP_K):
        wt = jnp.where(lane == k, es[k] * inv, wt)
    idx_ref[...] = idx_out.astype(I32)
    wt_ref[...] = wt


def _router(x, g, w_split, b_pad, *, tm):
    m, k = x.shape
    return pl.pallas_call(
        _router_body,
        grid=(m // tm,),
        in_specs=[
            pl.BlockSpec((tm, k), lambda i: (i, 0)),
            pl.BlockSpec((1, k), lambda i: (0, 0)),
            pl.BlockSpec((2, k, LANES), lambda i: (0, 0, 0)),
            pl.BlockSpec((1, LANES), lambda i: (0, 0)),
        ],
        out_specs=[
            pl.BlockSpec((tm, LANES), lambda i: (i, 0)),
            pl.BlockSpec((tm, LANES), lambda i: (i, 0)),
        ],
        out_shape=[
            jax.ShapeDtypeStruct((m, LANES), I32),
            jax.ShapeDtypeStruct((m, LANES), F32),
        ],
        compiler_params=_params("parallel"),
        name="router",
    )(x, g.reshape(1, k), w_split, b_pad)


def _row_slice(t, pitch=SEGS):
    return pl.ds(pl.multiple_of(t * pitch, 8), SEGS)


def _col_block(s, n_tokens, first_token=0, pitch=SEGS):
    return pl.ds(first_token * pitch + s, n_tokens, stride=pitch)


def _stage_indices(idx_hbm, idx_smem, sem, tile):
    n = idx_smem.shape[0]
    cp = pltpu.make_async_copy(idx_hbm.at[pl.ds(pl.multiple_of(tile * n, n), n)], idx_smem, sem)
    cp.start()
    cp.wait()


def _start_rows(n, row_copy):
    def start_pair(i, c):
        row_copy(2 * i).start(priority=0)
        row_copy(2 * i + 1).start(priority=1)
        return c

    lax.fori_loop(0, n // 2, start_pair, 0, unroll=4)


def _dispatch_body(dest_hbm, x_ref, g_ref, rows_hbm, idx_smem, stage, sem_idx, sem):
    _stage_indices(dest_hbm, idx_smem, sem_idx, pl.program_id(0))
    h = _rms(x_ref[...], g_ref[...])
    tm = h.shape[0]
    for s in range(SEGS):
        stage[_col_block(s, tm, pitch=ROW_PITCH), :] = h[:, s * LANES:(s + 1) * LANES]
    shift = TOP_K.bit_length() - 1

    def row_copy(a):
        return pltpu.make_async_copy(stage.at[_row_slice(jnp.right_shift(a, shift), ROW_PITCH)],
                                     rows_hbm.at[_row_slice(idx_smem[a])], sem)

    n = idx_smem.shape[0]
    _start_rows(n, row_copy)
    pltpu.make_async_copy(rows_hbm.at[pl.ds(0, n * SEGS)], rows_hbm.at[pl.ds(n * SEGS, n * SEGS)], sem).wait()


def _dispatch(dest, x, g, n_slots, *, tm):
    m, d = x.shape
    return pl.pallas_call(
        _dispatch_body,
        grid_spec=pltpu.PrefetchScalarGridSpec(
            num_scalar_prefetch=0,
            grid=(m // tm,),
            in_specs=[
                pl.BlockSpec(memory_space=pl.ANY),
                pl.BlockSpec((tm, d), lambda i: (i, 0)),
                pl.BlockSpec((1, d), lambda i: (0, 0)),
            ],
            out_specs=pl.BlockSpec(memory_space=pl.ANY),
            scratch_shapes=[
                pltpu.SMEM((TOP_K * tm,), I32),
                pltpu.VMEM((tm * ROW_PITCH, LANES), F32),
                pltpu.SemaphoreType.DMA,
                pltpu.SemaphoreType.DMA,
            ],
        ),
        out_shape=jax.ShapeDtypeStruct((n_slots * SEGS, LANES), F32),
        compiler_params=_params("arbitrary"),
        name="moe_dispatch",
    )(dest, x, g.reshape(1, d))


def _experts_body(blk_ref, exp_ref, rows_ref, x_ref, wg_ref, wl_ref, bg_ref, bl_ref, wd_ref, bd_ref, y_hbm,
                  xs_scr, acc_scr, y_stage, sem):
    c = pl.program_id(0)
    j = pl.program_id(1)
    last_c = pl.num_programs(0) - 1
    last_j = pl.num_programs(1) - 1

    def live_subs(chunk):
        return lax.div(rows_ref[chunk] + (MOE_SUB - 1), MOE_SUB)

    rows = rows_ref[c]

    def y_copy(chunk, r):
        first = (blk_ref[chunk] * MOE_CHUNK + r * MOE_SUB) * SEGS
        return pltpu.make_async_copy(y_stage.at[pl.ds(r * MOE_SUB * SEGS, MOE_SUB * SEGS)],
                                     y_hbm.at[pl.ds(pl.multiple_of(first, MOE_SUB * SEGS), MOE_SUB * SEGS)], sem)

    def for_live_subs(chunk, fn):
        n_live = live_subs(chunk)
        for r in range(MOE_CHUNK // MOE_SUB):
            @pl.when(r < n_live)
            def _():
                fn(y_copy(chunk, r))

    @pl.when(jnp.logical_and(j == 0, rows > 0))
    def _():
        for r in range(MOE_CHUNK // MOE_SUB):
            live = r * MOE_SUB + lax.broadcasted_iota(I32, (MOE_SUB, LANES), 0) < rows
            for s in range(SEGS):
                seg = jnp.where(live, x_ref[_col_block(s, MOE_SUB, r * MOE_SUB), :], 0.0)
                xs_scr[r * MOE_SUB:(r + 1) * MOE_SUB, s * LANES:(s + 1) * LANES] = seg.astype(BF16)

    def hidden():
        xs = xs_scr[...]
        gate = jnp.minimum(_dot(xs, wg_ref[0].astype(BF16)) + bg_ref[0], SWIGLU_LIMIT)
        lin = jnp.clip(_dot(xs, wl_ref[0].astype(BF16)) + bl_ref[0], -SWIGLU_LIMIT, SWIGLU_LIMIT)
        return (gate * jax.nn.sigmoid(SWIGLU_ALPHA * gate) * (lin + 1.0)).astype(BF16)

    def down_tiles(act):
        for n in range(acc_scr.shape[1] // MXU_WIDTH):
            cols = slice(n * MXU_WIDTH, (n + 1) * MXU_WIDTH)
            yield cols, _dot(act, wd_ref[0, :, cols].astype(BF16))

    @pl.when(jnp.logical_and(j == 0, rows > 0))
    def _():
        for cols, out in down_tiles(hidden()):
            acc_scr[:, cols] = bd_ref[0, :, cols] + out

    @pl.when(jnp.logical_and(jnp.logical_and(j > 0, j < last_j), rows > 0))
    def _():
        for cols, out in down_tiles(hidden()):
            acc_scr[:, cols] += out

    @pl.when(jnp.logical_and(j == last_j, rows > 0))
    def _():
        @pl.when(c > 0)
        def _():
            for_live_subs(c - 1, lambda cp: cp.wait())

        for cols, out in down_tiles(hidden()):
            y = acc_scr[:, cols] + out
            for i in range(MXU_WIDTH // LANES):
                s = cols.start // LANES + i
                y_stage[_col_block(s, MOE_CHUNK), :] = y[:, i * LANES:(i + 1) * LANES]
        for_live_subs(c, lambda cp: cp.start())


    @pl.when(jnp.logical_and(j == last_j, c == last_c))
    def _():
        for_live_subs(blk_ref[c], lambda cp: cp.wait())


def _experts(chunk_blk, chunk_expert, chunk_rows, x_rows, w_gate_up, b_gate_up, w_down, b_down):
    n_chunks = chunk_blk.shape[0]
    d = SEGS * LANES
    n_j = D_FF // MOE_TN
    return pl.pallas_call(
        _experts_body,
        grid_spec=pltpu.PrefetchScalarGridSpec(
            num_scalar_prefetch=3,
            grid=(n_chunks, n_j),
            in_specs=[
                pl.BlockSpec((MOE_CHUNK * SEGS, LANES), lambda c, j, blk, ex, rw: (blk[c], 0)),
                pl.BlockSpec((1, d, MOE_TN), lambda c, j, blk, ex, rw: (ex[c], 0, j)),
                pl.BlockSpec((1, d, MOE_TN), lambda c, j, blk, ex, rw: (ex[c], 0, n_j + j)),
                pl.BlockSpec((1, 1, MOE_TN), lambda c, j, blk, ex, rw: (ex[c], 0, j)),
                pl.BlockSpec((1, 1, MOE_TN), lambda c, j, blk, ex, rw: (ex[c], 0, n_j + j)),
                pl.BlockSpec((1, MOE_TN, d), lambda c, j, blk, ex, rw: (ex[c], j, 0)),
                pl.BlockSpec((1, 1, d), lambda c, j, blk, ex, rw: (ex[c], 0, 0)),
            ],
            out_specs=pl.BlockSpec(memory_space=pl.ANY),
            scratch_shapes=[
                pltpu.VMEM((MOE_CHUNK, d), BF16),
                pltpu.VMEM((MOE_CHUNK, d), F32),
                pltpu.VMEM((MOE_CHUNK * SEGS, LANES), F32),
                pltpu.SemaphoreType.DMA,
            ],
        ),
        out_shape=jax.ShapeDtypeStruct((n_chunks * MOE_CHUNK * SEGS, LANES), F32),
        compiler_params=_params("arbitrary", "arbitrary"),
        name="moe_experts",
    )(chunk_blk, chunk_expert, chunk_rows, x_rows, w_gate_up, w_gate_up,
      b_gate_up.reshape(N_EXPERTS, 1, -1), b_gate_up.reshape(N_EXPERTS, 1, -1),
      w_down, b_down.reshape(N_EXPERTS, 1, -1))


def _combine_body(dest_hbm, x_ref, wt_ref, g_ref, rows_hbm, o_ref, idx_smem, ybuf, sem_idx, sems, *, final_norm):
    i = pl.program_id(0)
    n_tiles = pl.num_programs(0)
    shift = TOP_K.bit_length() - 1
    tm = x_ref.shape[0]
    n = idx_smem.shape[0]

    def gather(tile, slot):
        _stage_indices(dest_hbm, idx_smem, sem_idx, tile)

        def row_copy(a):
            row = jnp.bitwise_and(a, TOP_K - 1) * tm + jnp.right_shift(a, shift)
            return pltpu.make_async_copy(rows_hbm.at[_row_slice(idx_smem[a])],
                                         ybuf.at[slot, _row_slice(row, ROW_PITCH)], sems.at[slot])

        _start_rows(n, row_copy)

    def combine(slot):
        if slot == 0:
            @pl.when(i == 0)
            def _():
                gather(0, 0)

        @pl.when(i + 1 < n_tiles)
        def _():
            gather(i + 1, 1 - slot)

        pltpu.make_async_copy(rows_hbm.at[pl.ds(0, n * SEGS)], ybuf.at[slot, pl.ds(0, n * SEGS)], sems.at[slot]).wait()

        wt = wt_ref[...]
        segs = []
        for s in range(SEGS):
            acc = x_ref[:, s * LANES:(s + 1) * LANES]
            for k in range(TOP_K):
                acc = acc + ybuf[slot, _col_block(s, tm, k * tm, ROW_PITCH), :] * wt[:, k:k + 1]
            segs.append(acc)
        if final_norm:
            ssq = functools.reduce(lambda a, c: a + c, [jnp.sum(v * v, axis=-1, keepdims=True) for v in segs])
            inv = lax.rsqrt(ssq / (SEGS * LANES) + EPS)
            segs = [v * inv * g_ref[:, s * LANES:(s + 1) * LANES] for s, v in enumerate(segs)]
        for s, v in enumerate(segs):
            o_ref[:, s * LANES:(s + 1) * LANES] = v

    for slot in range(2):
        @pl.when(jnp.bitwise_and(i, 1) == slot)
        def _():
            combine(slot)


def _combine(dest, x, wt, g, y_rows, *, tm, final_norm):
    m, d = x.shape
    return pl.pallas_call(
        functools.partial(_combine_body, final_norm=final_norm),
        grid_spec=pltpu.PrefetchScalarGridSpec(
            num_scalar_prefetch=0,
            grid=(m // tm,),
            in_specs=[
                pl.BlockSpec(memory_space=pl.ANY),
                pl.BlockSpec((tm, d), lambda i: (i, 0)),
                pl.BlockSpec((tm, LANES), lambda i: (i, 0)),
                pl.BlockSpec((1, d), lambda i: (0, 0)),
                pl.BlockSpec(memory_space=pl.ANY),
            ],
            out_specs=pl.BlockSpec((tm, d), lambda i: (i, 0)),
            scratch_shapes=[
                pltpu.SMEM((TOP_K * tm,), I32),
                pltpu.VMEM((2, TOP_K * tm * ROW_PITCH, LANES), F32),
                pltpu.SemaphoreType.DMA,
                pltpu.SemaphoreType.DMA((2,)),
            ],
        ),
        out_shape=jax.ShapeDtypeStruct((m, d), F32),
        compiler_params=_params("arbitrary"),
        name="moe_combine",
    )(dest, x, wt, g.reshape(1, d), y_rows)


def _routing_tables(top_idx, n_chunks):
    expert_flat = top_idx.reshape(-1)
    onehot = (expert_flat[:, None] == jnp.arange(N_EXPERTS, dtype=I32)[None, :]).astype(I32)
    running = jnp.cumsum(onehot, axis=0)
    rank = jnp.sum(onehot * running, axis=1) - 1
    counts = running[-1]
    chunks_per_expert = (counts + MOE_CHUNK - 1) // MOE_CHUNK
    chunk_end = jnp.cumsum(chunks_per_expert)
    chunk_start = chunk_end - chunks_per_expert
    dest = (chunk_start * MOE_CHUNK)[expert_flat] + rank

    c = jnp.arange(n_chunks, dtype=I32)
    used = chunk_end[-1]
    c_eff = jnp.minimum(c, used - 1)
    chunk_expert = jnp.sum((chunk_end[None, :] <= c_eff[:, None]).astype(I32), axis=1)
    rows = jnp.clip(counts[chunk_expert] - (c_eff - chunk_start[chunk_expert]) * MOE_CHUNK, 0, MOE_CHUNK)
    chunk_rows = jnp.where(c < used, rows, 0).astype(I32)
    return dest.astype(I32), c_eff.astype(I32), chunk_expert, chunk_rows


def kernel(x, mem, norm_mix_g, w_in, w_gla_gate_up, b_gla_gate, sb_norm_g, gla_norm_g, w_out,
           norm_cross_g, norm_mem_g, w_cq, w_ckv, w_co,
           norm_moe_g, w_router, b_router, w_gate_up, b_gate_up, w_down, b_down,
           norm_final_g):
    batch, seq, d = x.shape
    n_tok = batch * seq
    depth = w_in.shape[0]
    xf = x.reshape(n_tok, d)
    mem_f = mem.reshape(-1, d)
    mem_len = mem.shape[1]

    for layer in range(depth):
        w_in_l = w_in[layer]
        w_main = w_in_l
        w_low = jnp.pad(w_in_l[:, PROJ_MAIN:], ((0, 0), (0, LANES - GLA_GATE_RANK))).astype(BF16)
        w_up = jnp.pad(w_gla_gate_up[layer], ((0, LANES - GLA_GATE_RANK), (0, 0))).astype(BF16)
        proj, log_a = _inproj(xf, norm_mix_g[layer], w_main, w_low, w_up, b_gla_gate[layer], tm=1024, tn=1024)
        proj = proj.reshape(batch, seq, PROJ_MAIN)
        sb_o = _sb_attention(proj, sb_norm_g[layer], tq=256, heads=4)
        gla_o = _gla(proj, log_a.reshape(batch, seq, GLA_K_WIDTH), gla_norm_g[layer], rows=256)
        xf = _mix_out(sb_o.reshape(n_tok, SB_WIDTH), gla_o.reshape(n_tok, GLA_V_WIDTH),
                      w_out[layer].astype(BF16), xf, tm=1024, tn=1024)

        q = _norm_matmul(xf, norm_cross_g[layer], w_cq[layer].astype(BF16), tm=1024, tn=1024, out_dtype=BF16)
        kv = _norm_matmul(mem_f, norm_mem_g[layer], w_ckv[layer].astype(BF16), tm=1024, tn=1024, out_dtype=BF16)
        xf = _cross_attention(q, kv.reshape(batch, mem_len, 2 * d), w_co[layer].astype(BF16), xf,
                              seq=seq, tm=1024, tn=1024)

        w_r = jnp.pad(w_router[layer], ((0, 0), (0, LANES - N_EXPERTS)))
        w_r_hi = w_r.astype(BF16)
        w_r_lo = (w_r - w_r_hi.astype(F32)).astype(BF16)
        b_r = jnp.pad(b_router[layer], (0, LANES - N_EXPERTS), constant_values=-1e30).reshape(1, LANES)
        top_idx, top_w = _router(xf, norm_moe_g[layer], jnp.stack([w_r_hi, w_r_lo]), b_r, tm=512)

        n_chunks = n_tok * TOP_K // MOE_CHUNK + N_EXPERTS
        dest, chunk_blk, chunk_expert, chunk_rows = _routing_tables(top_idx[:, :TOP_K], n_chunks)
        x_rows = _dispatch(dest, xf, norm_moe_g[layer], n_chunks * MOE_CHUNK, tm=512)
        y_rows = _experts(chunk_blk, chunk_expert, chunk_rows, x_rows,
                          w_gate_up[layer], b_gate_up[layer], w_down[layer], b_down[layer])
        xf = _combine(dest, xf, top_w, norm_final_g, y_rows, tm=256, final_norm=layer + 1 == depth)

    return xf.reshape(batch, seq, d)
```
